```python
import math
import jax, jax.numpy as jnp
from jax import lax
import numpy as np

D_MODEL = 1024
BATCH = 8
SEQ = 4096
DEPTH = 2
DEC_BATCH = 16
DEC_SEQ = 2048
PAST_LEN = 128

N_MEM = 256
EPS = 1e-6
NEG_INF = -1e30
DIL_GROUPS = ((128, 1), (512, 4), (2048, 16))
N_DIL = 3
DIL_HEADS = 4
DIL_HEAD_DIM = 128
DIL_WIDTH = N_DIL * DIL_HEADS * DIL_HEAD_DIM
DIL_OUT = DIL_HEADS * DIL_HEAD_DIM
MLA_HEADS = 8
MLA_Q_LORA = 384
MLA_KV_LORA = 256
MLA_NOPE = 64
MLA_ROPE = 32
MLA_QK = MLA_NOPE + MLA_ROPE
MLA_V = 64
MLA_OUT = MLA_HEADS * MLA_V
ROPE_THETA = 10000.0
Q_BLOCK = 128
MEM_HEADS = 4
MEM_HEAD_DIM = 128
MEM_WIDTH = MEM_HEADS * MEM_HEAD_DIM
N_BRANCH = 3
BRANCH_WIDTH = 512
D_FF = 4 * D_MODEL
IN_SPLITS = (DIL_WIDTH, DIL_WIDTH, DIL_WIDTH, MLA_Q_LORA, MLA_KV_LORA, MLA_ROPE, MEM_WIDTH, N_BRANCH * D_MODEL)
D_IN = 3 * DIL_WIDTH + MLA_Q_LORA + MLA_KV_LORA + MLA_ROPE + MEM_WIDTH + N_BRANCH * D_MODEL

kernel_name = "hybrid_dilated_mla_memory_encoder"


def rmsnorm(x, g):
    xf = x.astype(jnp.float32)
    r = lax.rsqrt(jnp.mean(xf * xf, axis=-1, keepdims=True) + EPS)
    return (xf * r).astype(x.dtype) * g


def alibi_slopes():
    n = N_DIL * DIL_HEADS
    s = np.array([2.0 ** (-8.0 * (k + 1) / n) for k in range(n)], dtype=np.float32)
    return jnp.asarray(s.reshape(N_DIL, DIL_HEADS))


def split_cols(z):
    outs, off = [], 0
    for w in IN_SPLITS:
        outs.append(z[..., off:off + w])
        off += w
    return outs


def rope(x, S):
    half = MLA_ROPE // 2
    inv = ROPE_THETA ** (-jnp.arange(half, dtype=jnp.float32) * 2.0 / MLA_ROPE)
    ang = jnp.arange(S, dtype=jnp.float32)[:, None] * inv[None, :]
    c = jnp.cos(ang)[None, :, None, :].astype(x.dtype)
    s = jnp.sin(ang)[None, :, None, :].astype(x.dtype)
    x1, x2 = x[..., :half], x[..., half:]
    return jnp.concatenate([x1 * c - x2 * s, x1 * s + x2 * c], axis=-1)


def dilated_group(q, k, v, slopes, window, dilation):
    B, S, H, Dh = q.shape
    R = window // (2 * dilation)
    L = S // dilation
    nb = -(-L // R)
    Lp = nb * R

    def by_residue(t):
        t = t.reshape(B, L, dilation, H, Dh).transpose(0, 2, 1, 3, 4)
        return jnp.pad(t, ((0, 0), (0, 0), (0, Lp - L), (0, 0), (0, 0)))

    def windows(t):
        t = jnp.pad(by_residue(t), ((0, 0), (0, 0), (R, R), (0, 0), (0, 0)))
        t = t.reshape(B, dilation, nb + 2, R, H, Dh)
        return jnp.concatenate([t[:, :, :-2], t[:, :, 1:-1], t[:, :, 2:]], axis=3)

    qs = by_residue(q).reshape(B, dilation, nb, R, H, Dh)
    ks = windows(k)
    vs = windows(v)
    s = jnp.einsum('bgnqhd,bgnkhd->bghnqk', qs, ks).astype(jnp.float32) / math.sqrt(Dh)
    a = jnp.arange(R)[:, None]
    c = jnp.arange(3 * R)[None, :]
    rel = c - R - a
    ku = jnp.arange(nb)[:, None, None] * R - R + c[None]
    mask = (jnp.abs(rel)[None] <= R) & (ku >= 0) & (ku < L)
    dist = (jnp.abs(rel) * dilation).astype(jnp.float32)
    bias = -slopes.astype(jnp.float32)[:, None, None, None] * dist[None, None]
    s = jnp.where(mask, s + bias, NEG_INF)
    m = jnp.max(s, axis=-1, keepdims=True)
    p = jnp.exp(s - m)
    den = jnp.sum(p, axis=-1, keepdims=True)
    o = jnp.einsum('bghnqk,bgnkhd->bgnqhd', p.astype(v.dtype), vs).astype(jnp.float32)
    o = o / den[..., 0].transpose(0, 1, 3, 4, 2)[..., None]
    lse = (m + jnp.log(den))[..., 0].transpose(0, 1, 3, 4, 2)
    o = o.reshape(B, dilation, Lp, H, Dh)[:, :, :L].transpose(0, 2, 1, 3, 4).reshape(B, S, H, Dh)
    lse = lse.reshape(B, dilation, Lp, H)[:, :, :L].transpose(0, 2, 1, 3).reshape(B, S, H)
    return o, lse


def dilated_attention(q, k, v, q_g, k_g):
    B, S, _ = q.shape
    shp = (B, S, N_DIL, DIL_HEADS, DIL_HEAD_DIM)
    q = rmsnorm(q.reshape(shp), q_g)
    k = rmsnorm(k.reshape(shp), k_g)
    v = v.reshape(shp)
    slopes = alibi_slopes()
    outs, lses = [], []
    for g, (win, dil) in enumerate(DIL_GROUPS):
        o, l = dilated_group(q[:, :, g], k[:, :, g], v[:, :, g], slopes[g], win, dil)
        outs.append(o)
        lses.append(l)
    w = jax.nn.softmax(jnp.stack(lses, 0), axis=0)
    o = jnp.sum(w[..., None] * jnp.stack(outs, 0), axis=0)
    return o.reshape(B, S, DIL_OUT).astype(q.dtype)


def mla_attention(c_q, c_kv, k_rope, q_a_g, kv_a_g, w_q_b, w_kv_b, q_g, k_g):
    B, S, _ = c_q.shape
    q = (rmsnorm(c_q, q_a_g) @ w_q_b).reshape(B, S, MLA_HEADS, MLA_QK)
    kv = (rmsnorm(c_kv, kv_a_g) @ w_kv_b).reshape(B, S, MLA_HEADS, MLA_NOPE + MLA_V)
    k_nope, v = kv[..., :MLA_NOPE], kv[..., MLA_NOPE:]
    k = jnp.concatenate([k_nope, jnp.broadcast_to(k_rope[:, :, None, :], (B, S, MLA_HEADS, MLA_ROPE))], -1)
    q = rmsnorm(q, q_g)
    k = rmsnorm(k, k_g)
    q = jnp.concatenate([q[..., :MLA_NOPE], rope(q[..., MLA_NOPE:], S)], -1)
    k = jnp.concatenate([k[..., :MLA_NOPE], rope(k[..., MLA_NOPE:], S)], -1)
    scale = 1.0 / math.sqrt(MLA_QK)
    nq = S // Q_BLOCK
    qb = q.reshape(B, nq, Q_BLOCK, MLA_HEADS, MLA_QK).transpose(1, 0, 2, 3, 4)

    def block(qi):
        s = jnp.einsum('bqhd,bkhd->bhqk', qi, k).astype(jnp.float32) * scale
        p = jax.nn.softmax(s, axis=-1).astype(v.dtype)
        return jnp.einsum('bhqk,bkhd->bqhd', p, v)

    o = lax.map(block, qb)
    return o.transpose(1, 0, 2, 3, 4).reshape(B, S, MLA_OUT)


def memory_attention(q, mem, mem_g, w_mem_kv, q_g, k_g):
    B, S, _ = q.shape
    M = mem.shape[1]
    kv = rmsnorm(mem, mem_g) @ w_mem_kv
    k = rmsnorm(kv[..., :MEM_WIDTH].reshape(B, M, MEM_HEADS, MEM_HEAD_DIM), k_g)
    v = kv[..., MEM_WIDTH:].reshape(B, M, MEM_HEADS, MEM_HEAD_DIM)
    q = rmsnorm(q.reshape(B, S, MEM_HEADS, MEM_HEAD_DIM), q_g)
    s = jnp.einsum('bshd,bmhd->bhsm', q, k).astype(jnp.float32) / math.sqrt(MEM_HEAD_DIM)
    p = jax.nn.softmax(s, axis=-1).astype(v.dtype)
    return jnp.einsum('bhsm,bmhd->bshd', p, v).reshape(B, S, MEM_WIDTH)


def encoder_layer(x, mem, l, mix_norm, w_in, dil_q_norm, dil_k_norm, mla_q_a_norm, mla_kv_a_norm,
                  w_mla_q_b, w_mla_kv_b, mla_q_norm, mla_k_norm, mem_norm, w_mem_kv, mem_q_norm,
                  mem_k_norm, w_branch, w_out, ffn_norm, w_ff1, w_ff2):
    B, S, _ = x.shape
    h = rmsnorm(x, mix_norm[l])
    z = h @ w_in[l]
    dq, dk, dv, c_q, c_kv, k_rope, mq, gl = split_cols(z)
    a = dilated_attention(dq, dk, dv, dil_q_norm[l], dil_k_norm[l])
    b = mla_attention(c_q, c_kv, k_rope, mla_q_a_norm[l], mla_kv_a_norm[l], w_mla_q_b[l], w_mla_kv_b[l],
                      mla_q_norm[l], mla_k_norm[l])
    m = memory_attention(mq, mem, mem_norm[l], w_mem_kv[l], mem_q_norm[l], mem_k_norm[l])
    br = jnp.stack([a, b, m], axis=2)
    proj = jnp.einsum('bsnc,ncd->bsnd', br, w_branch[l])
    gates = jax.nn.sigmoid(gl.reshape(B, S, N_BRANCH, D_MODEL))
    x = x + jnp.sum(gates * proj, axis=2) @ w_out[l]
    h2 = rmsnorm(x, ffn_norm[l])
    x = x + jnp.square(jax.nn.relu(h2 @ w_ff1[l])) @ w_ff2[l]
    return x


def setup_inputs(seed: int = 0) -> dict:
    key = jax.random.key(seed)
    ks = jax.random.split(key, 24)
    f32 = jnp.float32

    def w(k, shape, fan_in):
        return jax.random.normal(k, shape, f32) * (fan_in ** -0.5)

    def gain(k, shape):
        return 1.0 + 0.02 * jax.random.normal(k, shape, f32)

    return {
        "x_prompt": jax.random.normal(ks[0], (BATCH, SEQ, D_MODEL), f32),
        "x_sample": jax.random.normal(ks[1], (DEC_BATCH, DEC_SEQ, D_MODEL), f32),
        "mem_prompt": jax.random.normal(ks[2], (BATCH, N_MEM, D_MODEL), f32),
        "mem_sample": jax.random.normal(ks[3], (DEC_BATCH, N_MEM, D_MODEL), f32),
        "mix_norm": gain(ks[4], (DEPTH, D_MODEL)),
        "w_in": w(ks[5], (DEPTH, D_MODEL, D_IN), D_MODEL),
        "dil_q_norm": gain(ks[6], (DEPTH, N_DIL, DIL_HEADS, DIL_HEAD_DIM)),
        "dil_k_norm": gain(ks[7], (DEPTH, N_DIL, DIL_HEADS, DIL_HEAD_DIM)),
        "mla_q_a_norm": gain(ks[8], (DEPTH, MLA_Q_LORA)),
        "mla_kv_a_norm": gain(ks[9], (DEPTH, MLA_KV_LORA)),
        "w_mla_q_b": w(ks[10], (DEPTH, MLA_Q_LORA, MLA_HEADS * MLA_QK), MLA_Q_LORA),
        "w_mla_kv_b": w(ks[11], (DEPTH, MLA_KV_LORA, MLA_HEADS * (MLA_NOPE + MLA_V)), MLA_KV_LORA),
        "mla_q_norm": gain(ks[12], (DEPTH, MLA_QK)),
        "mla_k_norm": gain(ks[13], (DEPTH, MLA_QK)),
        "mem_norm": gain(ks[14], (DEPTH, D_MODEL)),
        "w_mem_kv": w(ks[15], (DEPTH, D_MODEL, 2 * MEM_WIDTH), D_MODEL),
        "mem_q_norm": gain(ks[16], (DEPTH, MEM_HEAD_DIM)),
        "mem_k_norm": gain(ks[17], (DEPTH, MEM_HEAD_DIM)),
        "w_branch": w(ks[18], (DEPTH, N_BRANCH, BRANCH_WIDTH, D_MODEL), BRANCH_WIDTH),
        "w_out": w(ks[19], (DEPTH, D_MODEL, D_MODEL), D_MODEL),
        "ffn_norm": gain(ks[20], (DEPTH, D_MODEL)),
        "w_ff1": w(ks[21], (DEPTH, D_MODEL, D_FF), D_MODEL),
        "w_ff2": w(ks[22], (DEPTH, D_FF, D_MODEL), D_FF),
    }


def reference(x_prompt, x_sample, mem_prompt, mem_sample, mix_norm, w_in, dil_q_norm, dil_k_norm,
              mla_q_a_norm, mla_kv_a_norm, w_mla_q_b, w_mla_kv_b, mla_q_norm, mla_k_norm, mem_norm,
              w_mem_kv, mem_q_norm, mem_k_norm, w_branch, w_out, ffn_norm, w_ff1, w_ff2):
    yp = x_prompt
    ys = x_sample
    for l in range(DEPTH):
        yp = encoder_layer(yp, mem_prompt, l, mix_norm, w_in, dil_q_norm, dil_k_norm, mla_q_a_norm,
                           mla_kv_a_norm, w_mla_q_b, w_mla_kv_b, mla_q_norm, mla_k_norm, mem_norm,
                           w_mem_kv, mem_q_norm, mem_k_norm, w_branch, w_out, ffn_norm, w_ff1, w_ff2)
        ys = encoder_layer(ys, mem_sample, l, mix_norm, w_in, dil_q_norm, dil_k_norm, mla_q_a_norm,
                           mla_kv_a_norm, w_mla_q_b, w_mla_kv_b, mla_q_norm, mla_k_norm, mem_norm,
                           w_mem_kv, mem_q_norm, mem_k_norm, w_branch, w_out, ffn_norm, w_ff1, w_ff2)
    return (yp, ys)
```

```python
import functools
import math

import numpy as np
import jax
import jax.numpy as jnp
from jax import lax
from jax.experimental import pallas as pl
from jax.experimental.pallas import tpu as pltpu

F32 = jnp.float32
BF16 = jnp.bfloat16

LANE = 128
VMEM_LIMIT_BYTES = 56 * 1024 * 1024

D_MODEL = 1024
EPS = 1e-6
NEG_INF = -1e30
LOG2E = math.log2(math.e)
DIL_GROUPS = ((128, 1), (512, 4), (2048, 16))
N_DIL = 3
DIL_HEADS = 4
DIL_HEAD_DIM = 128
DIL_WIDTH = N_DIL * DIL_HEADS * DIL_HEAD_DIM
DIL_OUT = DIL_HEADS * DIL_HEAD_DIM
MLA_HEADS = 8
MLA_Q_LORA = 384
MLA_KV_LORA = 256
MLA_NOPE = 64
MLA_ROPE = 32
MLA_QK = MLA_NOPE + MLA_ROPE
MLA_V = 64
MLA_PAD = MLA_HEADS * LANE
ROPE_THETA = 10000.0
MEM_HEADS = 4
MEM_HEAD_DIM = 128
MEM_WIDTH = MEM_HEADS * MEM_HEAD_DIM
N_BRANCH = 3
BRANCH_WIDTH = 512
D_FF = 4 * D_MODEL
GATE_WIDTH = N_BRANCH * D_MODEL

COL_DQ = 0
COL_DK = COL_DQ + DIL_WIDTH
COL_DV = COL_DK + DIL_WIDTH
COL_CQ = COL_DV + DIL_WIDTH
COL_CKV = COL_CQ + MLA_Q_LORA
COL_KR = COL_CKV + MLA_KV_LORA
COL_MQ = COL_KR + LANE
COL_GL = COL_MQ + MEM_WIDTH
COL_END = COL_GL + GATE_WIDTH

DIL_R = 64
DIL_SUB = 128
DIL_QSCALE = LOG2E / math.sqrt(DIL_HEAD_DIM)
MLA_QSCALE = LOG2E / math.sqrt(MLA_QK)
MEM_QSCALE = LOG2E / math.sqrt(MEM_HEAD_DIM)

TOKEN_TILE = 256
MLA_TQ = 256
MLA_TK = 256
MEM_TQ = 512
FF_CHUNK = 1024

NT_DIMS = (((1,), (1,)), ((), ()))


def _alibi_slopes():
    n = N_DIL * DIL_HEADS
    s = np.array([2.0 ** (-8.0 * (k + 1) / n) for k in range(n)], dtype=np.float32)
    return s.reshape(N_DIL, DIL_HEADS)


def _rinv(x, width):
    return lax.rsqrt(jnp.sum(x * x, axis=-1, keepdims=True) * (1.0 / width) + EPS)


def _resident(shape):
    nd = len(shape)
    return pl.BlockSpec(shape, lambda *_: (0,) * nd, pipeline_mode=pl.Buffered(1))


def _params(semantics):
    return pltpu.CompilerParams(dimension_semantics=semantics, vmem_limit_bytes=VMEM_LIMIT_BYTES)


def _rope(t, c, s1, s2):
    return t * c + pltpu.roll(t, LANE - MLA_ROPE // 2, 1) * s1 + pltpu.roll(t, MLA_ROPE // 2, 1) * s2


def _in_kernel(x_ref, gmix_ref, w_ref, gdq_ref, gdk_ref, gqa_ref, gkva_ref, wqb_ref, wkvb_ref,
               gmq_ref, gmk_ref, gmemq_ref, rc_ref, rs1_ref, rs2_ref,
               dq_ref, dk_ref, dv_ref, qm_ref, km_ref, vm_ref, mq_ref, gl_ref):
    x = x_ref[...]
    h = (x * _rinv(x, D_MODEL) * gmix_ref[...]).astype(BF16)

    def proj(lo, hi):
        return jnp.dot(h, w_ref[:, lo:hi], preferred_element_type=F32)

    def head_norm(z, gains_ref, out_ref, scale):
        for i in range(z.shape[1] // LANE):
            sl = slice(i * LANE, (i + 1) * LANE)
            blk = z[:, sl]
            out_ref[:, sl] = (blk * _rinv(blk, LANE) * gains_ref[:, sl] * scale).astype(BF16)

    head_norm(proj(COL_DQ, COL_DK), gdq_ref, dq_ref, DIL_QSCALE)
    head_norm(proj(COL_DK, COL_DV), gdk_ref, dk_ref, 1.0)
    dv_ref[...] = proj(COL_DV, COL_CQ).astype(BF16)
    head_norm(proj(COL_MQ, COL_GL), gmemq_ref, mq_ref, MEM_QSCALE)
    gl_ref[...] = proj(COL_GL, COL_END).astype(BF16)

    rc, rs1, rs2 = rc_ref[...], rs1_ref[...], rs2_ref[...]

    cq = proj(COL_CQ, COL_CKV)
    cq = (cq * _rinv(cq, MLA_Q_LORA) * gqa_ref[...]).astype(BF16)
    q = jnp.dot(cq, wqb_ref[...], preferred_element_type=F32)
    ckv = proj(COL_CKV, COL_KR)
    ckv = (ckv * _rinv(ckv, MLA_KV_LORA) * gkva_ref[...]).astype(BF16)
    kv = jnp.dot(ckv, wkvb_ref[...], preferred_element_type=F32)
    k_rope = proj(COL_KR, COL_MQ)
    ones_lane = (lax.broadcasted_iota(jnp.int32, (1, LANE), 1) == MLA_V).astype(F32)
    for hd in range(MLA_HEADS):
        sl = slice(hd * LANE, (hd + 1) * LANE)
        qh = q[:, sl]
        qh = qh * _rinv(qh, MLA_QK) * gmq_ref[...]
        qm_ref[:, sl] = (_rope(qh, rc, rs1, rs2) * MLA_QSCALE).astype(BF16)
        kh = kv[:, sl] + k_rope
        kh = kh * _rinv(kh, MLA_QK) * gmk_ref[...]
        km_ref[:, sl] = _rope(kh, rc, rs1, rs2).astype(BF16)
        vm_ref[:, sl] = (kv[:, MLA_PAD + hd * LANE:MLA_PAD + (hd + 1) * LANE] + ones_lane).astype(BF16)


def _in_proj(xf, p, seq):
    tokens = xf.shape[0]
    tm = TOKEN_TILE
    pos_blocks = seq // tm
    row = lambda w: pl.BlockSpec((tm, w), lambda i: (i, 0))
    rope_spec = pl.BlockSpec((tm, LANE), lambda i: (i % pos_blocks, 0))
    out_widths = (DIL_WIDTH, DIL_WIDTH, DIL_WIDTH, MLA_PAD, MLA_PAD, MLA_PAD, MEM_WIDTH, GATE_WIDTH)
    return pl.pallas_call(
        _in_kernel,
        grid=(tokens // tm,),
        in_specs=[row(D_MODEL), _resident((1, D_MODEL)), _resident((D_MODEL, COL_END)),
                  _resident((1, DIL_WIDTH)), _resident((1, DIL_WIDTH)),
                  _resident((1, MLA_Q_LORA)), _resident((1, MLA_KV_LORA)),
                  _resident((MLA_Q_LORA, MLA_PAD)), _resident((MLA_KV_LORA, 2 * MLA_PAD)),
                  _resident((1, LANE)), _resident((1, LANE)), _resident((1, MEM_WIDTH)),
                  rope_spec, rope_spec, rope_spec],
        out_specs=[row(w) for w in out_widths],
        out_shape=[jax.ShapeDtypeStruct((tokens, w), BF16) for w in out_widths],
        compiler_params=_params(("parallel",)),
        name="in_proj",
    )(xf, p["g_mix"], p["w_in"], p["g_dq"], p["g_dk"], p["g_qa"], p["g_kva"], p["w_qb"], p["w_kvb"],
      p["g_mq"], p["g_mk"], p["g_memq"], p["rope_c"], p["rope_s1"], p["rope_s2"])


def _dil_kernel(*refs, slopes, dilation, sub_len, tl, has_prev, last):
    if has_prev:
        q_ref, kp_ref, kc_ref, kn_ref, vp_ref, vc_ref, vn_ref, oin_ref, lin_ref = refs[:9]
        outs = refs[9:]
    else:
        q_ref, kp_ref, kc_ref, kn_ref, vp_ref, vc_ref, vn_ref = refs[:7]
        outs = refs[7:]
    if last:
        o_ref, kcat, vcat = outs
    else:
        o_ref, l_ref, kcat, vcat = outs

    kcat[0:DIL_R, :] = kp_ref[...]
    kcat[DIL_R:DIL_R + tl, :] = kc_ref[...]
    kcat[DIL_R + tl:, :] = kn_ref[...]
    vcat[0:DIL_R, :] = vp_ref[...]
    vcat[DIL_R:DIL_R + tl, :] = vc_ref[...]
    vcat[DIL_R + tl:, :] = vn_ref[...]

    nkeys = DIL_SUB + 2 * DIL_R
    qi = lax.broadcasted_iota(jnp.int32, (DIL_SUB, nkeys), 0)
    kj = lax.broadcasted_iota(jnp.int32, (DIL_SUB, nkeys), 1)
    absrel = jnp.abs(kj - DIL_R - qi)
    band = absrel <= DIL_R
    dist = absrel.astype(F32)
    block_start = pl.program_id(2) * tl

    for sb in range(tl // DIL_SUB):
        r0 = sb * DIL_SUB
        upos = kj + (block_start + (r0 - DIL_R))
        valid = band & (upos >= 0) & (upos < sub_len)
        for hd in range(DIL_HEADS):
            sl = slice(hd * LANE, (hd + 1) * LANE)
            q = q_ref[r0:r0 + DIL_SUB, sl]
            k = kcat[r0:r0 + nkeys, sl]
            v = vcat[r0:r0 + nkeys, sl]
            s = lax.dot_general(q, k, NT_DIMS, preferred_element_type=F32)
            s = jnp.where(valid, s + dist * (-float(slopes[hd]) * dilation * LOG2E), NEG_INF)
            m = jnp.max(s, axis=-1, keepdims=True)
            pr = jnp.exp2(s - m)
            den = jnp.sum(pr, axis=-1, keepdims=True)
            o = jnp.dot(pr.astype(BF16), v, preferred_element_type=F32) / den
            lse = m + jnp.log2(den)
            if has_prev:
                o_prev = oin_ref[r0:r0 + DIL_SUB, sl]
                l_prev = lin_ref[r0:r0 + DIL_SUB, sl]
                top = jnp.maximum(l_prev, lse)
                w_prev = jnp.exp2(l_prev - top)
                w_cur = jnp.exp2(lse - top)
                tot = w_prev + w_cur
                o = (w_prev * o_prev + w_cur * o) / tot
                lse = top + jnp.log2(tot)
            if last:
                o_ref[r0:r0 + DIL_SUB, sl] = o.astype(BF16)
            else:
                o_ref[r0:r0 + DIL_SUB, sl] = o
                l_ref[r0:r0 + DIL_SUB, sl] = jnp.broadcast_to(lse, (DIL_SUB, LANE))


def _dilated_group(g, dq, dk, dv, state, batch, seq):
    window, dilation = DIL_GROUPS[g]
    assert window // (2 * dilation) == DIL_R
    sub_len = seq // dilation
    assert sub_len * dilation == seq and sub_len % DIL_SUB == 0
    tl = min(sub_len, 2 * DIL_SUB)
    assert sub_len % tl == 0
    edge_blocks = sub_len // DIL_R
    per_tl = tl // DIL_R
    has_prev = state is not None
    last = g == N_DIL - 1

    view = lambda t, w: t.reshape(batch, sub_len, dilation * w)
    col = lambda r: r * N_DIL + g
    q_spec = pl.BlockSpec((None, tl, DIL_OUT), lambda b, r, i: (b, i, col(r)))
    prev_spec = pl.BlockSpec((None, DIL_R, DIL_OUT),
                             lambda b, r, i: (b, jnp.maximum(i * per_tl - 1, 0), col(r)))
    next_spec = pl.BlockSpec((None, DIL_R, DIL_OUT),
                             lambda b, r, i: (b, jnp.minimum((i + 1) * per_tl, edge_blocks - 1), col(r)))
    st_spec = pl.BlockSpec((None, tl, DIL_OUT), lambda b, r, i: (b, i, r))

    args = [view(dq, DIL_WIDTH)] + [view(dk, DIL_WIDTH)] * 3 + [view(dv, DIL_WIDTH)] * 3
    in_specs = [q_spec, prev_spec, q_spec, next_spec, prev_spec, q_spec, next_spec]
    if has_prev:
        args += [view(state[0], DIL_OUT), view(state[1], DIL_OUT)]
        in_specs += [st_spec, st_spec]
    st_shape = (batch, sub_len, dilation * DIL_OUT)
    if last:
        out_specs = [st_spec]
        out_shape = [jax.ShapeDtypeStruct(st_shape, BF16)]
    else:
        out_specs = [st_spec, st_spec]
        out_shape = [jax.ShapeDtypeStruct(st_shape, F32)] * 2
    outs = pl.pallas_call(
        functools.partial(_dil_kernel, slopes=_alibi_slopes()[g], dilation=dilation, sub_len=sub_len,
                          tl=tl, has_prev=has_prev, last=last),
        grid=(batch, dilation, sub_len // tl),
        in_specs=in_specs,
        out_specs=out_specs,
        out_shape=out_shape,
        scratch_shapes=[pltpu.VMEM((tl + 2 * DIL_R, DIL_OUT), BF16)] * 2,
        compiler_params=_params(("parallel", "parallel", "parallel")),
        name=f"dilated_g{g}",
    )(*args)
    return [o.reshape(batch * seq, DIL_OUT) for o in outs]


def _dilated(dq, dk, dv, batch, seq):
    state = None
    for g in range(N_DIL):
        state = _dilated_group(g, dq, dk, dv, state, batch, seq)
    return state[0]


def _mla_kernel(q_ref, k_ref, v_ref, o_ref, *, tq, tk, nk):
    lane = lax.broadcasted_iota(jnp.int32, (tq, LANE), 1)
    for pair in range(MLA_HEADS // 2):
        halves = []
        for hd in (2 * pair, 2 * pair + 1):
            sl = slice(hd * LANE, (hd + 1) * LANE)
            q = q_ref[:, sl]

            def body(j, carry, sl=sl, q=q):
                m, acc = carry
                rows = pl.ds(pl.multiple_of(j * tk, tk), tk)
                s = lax.dot_general(q, k_ref[rows, sl], NT_DIMS, preferred_element_type=F32)
                m_new = jnp.maximum(m, jnp.max(s, axis=-1, keepdims=True))
                alpha = jnp.exp2(m - m_new)
                pr = jnp.exp2(s - m_new).astype(BF16)
                acc = alpha * acc + jnp.dot(pr, v_ref[rows, sl], preferred_element_type=F32)
                return m_new, acc

            init = (jnp.full((tq, 1), -jnp.inf, F32), jnp.zeros((tq, LANE), F32))
            _, acc = lax.fori_loop(0, nk, body, init)
            halves.append(acc / acc[:, MLA_V:MLA_V + 1])
        packed = jnp.where(lane < MLA_V, halves[0], pltpu.roll(halves[1], MLA_V, 1))
        o_ref[:, pair * LANE:(pair + 1) * LANE] = packed.astype(BF16)


def _mla(qm, km, vm, batch, seq):
    tq, tk = MLA_TQ, MLA_TK
    view = lambda t: t.reshape(batch, seq, MLA_PAD)
    kv_spec = pl.BlockSpec((None, seq, MLA_PAD), lambda b, i: (b, 0, 0))
    out = pl.pallas_call(
        functools.partial(_mla_kernel, tq=tq, tk=tk, nk=seq // tk),
        grid=(batch, seq // tq),
        in_specs=[pl.BlockSpec((None, tq, MLA_PAD), lambda b, i: (b, i, 0)), kv_spec, kv_spec],
        out_specs=pl.BlockSpec((None, tq, MLA_HEADS * MLA_V), lambda b, i: (b, i, 0)),
        out_shape=jax.ShapeDtypeStruct((batch, seq, MLA_HEADS * MLA_V), BF16),
        compiler_params=_params(("parallel", "arbitrary")),
        name="latent_attn",
    )(view(qm), view(km), view(vm))
    return out.reshape(batch * seq, MLA_HEADS * MLA_V)


def _memkv_kernel(mem_ref, g_ref, w_ref, gk_ref, k_ref, v_ref):
    x = mem_ref[...]
    h = (x * _rinv(x, D_MODEL) * g_ref[...]).astype(BF16)
    kv = jnp.dot(h, w_ref[...], preferred_element_type=F32)
    for hd in range(MEM_HEADS):
        sl = slice(hd * LANE, (hd + 1) * LANE)
        blk = kv[:, sl]
        k_ref[:, sl] = (blk * _rinv(blk, MEM_HEAD_DIM) * gk_ref[...]).astype(BF16)
    v_ref[...] = kv[:, MEM_WIDTH:].astype(BF16)


def _mem_kv(mem, p):
    batch, n_mem, _ = mem.shape
    spec = lambda w: pl.BlockSpec((None, n_mem, w), lambda b: (b, 0, 0))
    return pl.pallas_call(
        _memkv_kernel,
        grid=(batch,),
        in_specs=[spec(D_MODEL), _resident((1, D_MODEL)), _resident((D_MODEL, 2 * MEM_WIDTH)),
                  _resident((1, LANE))],
        out_specs=[spec(MEM_WIDTH), spec(MEM_WIDTH)],
        out_shape=[jax.ShapeDtypeStruct((batch, n_mem, MEM_WIDTH), BF16)] * 2,
        compiler_params=_params(("parallel",)),
        name="memory_kv",
    )(mem, p["g_mem"], p["w_memkv"], p["g_memk"])


def _memattn_kernel(q_ref, k_ref, v_ref, o_ref):
    for hd in range(MEM_HEADS):
        sl = slice(hd * LANE, (hd + 1) * LANE)
        s = lax.dot_general(q_ref[:, sl], k_ref[:, sl], NT_DIMS, preferred_element_type=F32)
        m = jnp.max(s, axis=-1, keepdims=True)
        pr = jnp.exp2(s - m)
        den = jnp.sum(pr, axis=-1, keepdims=True)
        o = jnp.dot(pr.astype(BF16), v_ref[:, sl], preferred_element_type=F32) / den
        o_ref[:, sl] = o.astype(BF16)


def _mem_attn(mq, kmem, vmem, batch, seq):
    n_mem = kmem.shape[1]
    tq = MEM_TQ
    kv_spec = pl.BlockSpec((None, n_mem, MEM_WIDTH), lambda b, i: (b, 0, 0))
    q_spec = pl.BlockSpec((None, tq, MEM_WIDTH), lambda b, i: (b, i, 0))
    out = pl.pallas_call(
        _memattn_kernel,
        grid=(batch, seq // tq),
        in_specs=[q_spec, kv_spec, kv_spec],
        out_specs=q_spec,
        out_shape=jax.ShapeDtypeStruct((batch, seq, MEM_WIDTH), BF16),
        compiler_params=_params(("parallel", "parallel")),
        name="memory_attn",
    )(mq.reshape(batch, seq, MEM_WIDTH), kmem, vmem)
    return out.reshape(batch * seq, MEM_WIDTH)


def _post_kernel(x_ref, a_ref, b_ref, m_ref, gl_ref, wb_ref, wo_ref, gffn_ref, w1_ref, w2_ref, o_ref):
    mixed = None
    for i, br_ref in enumerate((a_ref, b_ref, m_ref)):
        proj = jnp.dot(br_ref[...], wb_ref[i], preferred_element_type=F32)
        gate = jax.nn.sigmoid(gl_ref[:, i * D_MODEL:(i + 1) * D_MODEL].astype(F32))
        mixed = gate * proj if mixed is None else mixed + gate * proj
    x1 = x_ref[...] + jnp.dot(mixed.astype(BF16), wo_ref[...], preferred_element_type=F32)
    h2 = (x1 * _rinv(x1, D_MODEL) * gffn_ref[...]).astype(BF16)
    acc = x1
    for c in range(D_FF // FF_CHUNK):
        sl = slice(c * FF_CHUNK, (c + 1) * FF_CHUNK)
        u = jnp.dot(h2, w1_ref[:, sl], preferred_element_type=F32)
        u = jnp.square(jnp.maximum(u, 0.0)).astype(BF16)
        acc = acc + jnp.dot(u, w2_ref[sl, :], preferred_element_type=F32)
    o_ref[...] = acc


def _post(xf, a, b, m, gl, p):
    tokens = xf.shape[0]
    tm = TOKEN_TILE
    row = lambda w: pl.BlockSpec((tm, w), lambda i: (i, 0))
    return pl.pallas_call(
        _post_kernel,
        grid=(tokens // tm,),
        in_specs=[row(D_MODEL), row(BRANCH_WIDTH), row(BRANCH_WIDTH), row(BRANCH_WIDTH), row(GATE_WIDTH),
                  _resident((N_BRANCH, BRANCH_WIDTH, D_MODEL)), _resident((D_MODEL, D_MODEL)),
                  _resident((1, D_MODEL)), _resident((D_MODEL, D_FF)), _resident((D_FF, D_MODEL))],
        out_specs=row(D_MODEL),
        out_shape=jax.ShapeDtypeStruct((tokens, D_MODEL), F32),
        compiler_params=_params(("parallel",)),
        name="merge_ffn",
    )(xf, a, b, m, gl, p["w_branch"], p["w_out"], p["g_ffn"], p["w_ff1"], p["w_ff2"])


def _rope_tables(seq):
    half = MLA_ROPE // 2
    inv = ROPE_THETA ** (-jnp.arange(half, dtype=F32) * 2.0 / MLA_ROPE)
    ang = jnp.arange(seq, dtype=F32)[:, None] * inv[None, :]
    c, s = jnp.cos(ang), jnp.sin(ang)
    z = lambda w: jnp.zeros((seq, w), F32)
    tail = LANE - MLA_QK
    rope_c = jnp.concatenate([jnp.ones((seq, MLA_NOPE), F32), c, c, z(tail)], axis=-1)
    rope_s1 = jnp.concatenate([z(MLA_NOPE), -s, z(half), z(tail)], axis=-1)
    rope_s2 = jnp.concatenate([z(MLA_NOPE), z(half), s, z(tail)], axis=-1)
    return rope_c, rope_s1, rope_s2


def _pad_lanes(t, width):
    return jnp.pad(t, [(0, 0)] * (t.ndim - 1) + [(0, width - t.shape[-1])])


def _prep_layer(l, mix_norm, w_in, dil_q_norm, dil_k_norm, mla_q_a_norm, mla_kv_a_norm, w_mla_q_b,
                w_mla_kv_b, mla_q_norm, mla_k_norm, mem_norm, w_mem_kv, mem_q_norm, mem_k_norm,
                w_branch, w_out, ffn_norm, w_ff1, w_ff2):
    w = w_in[l]
    o_cq = 3 * DIL_WIDTH
    o_ckv = o_cq + MLA_Q_LORA
    o_kr = o_ckv + MLA_KV_LORA
    o_mq = o_kr + MLA_ROPE
    o_gl = o_mq + MEM_WIDTH
    kr = jnp.pad(w[:, o_kr:o_mq], ((0, 0), (MLA_NOPE, LANE - MLA_QK)))
    w_r = jnp.concatenate([w[:, :o_kr], kr, w[:, o_mq:]], axis=1).astype(BF16)
    w_qb = _pad_lanes(w_mla_q_b[l].reshape(MLA_Q_LORA, MLA_HEADS, MLA_QK), LANE)
    w_kvb = w_mla_kv_b[l].reshape(MLA_KV_LORA, MLA_HEADS, MLA_NOPE + MLA_V)
    w_kb = _pad_lanes(w_kvb[:, :, :MLA_NOPE], LANE).reshape(MLA_KV_LORA, MLA_PAD)
    w_vb = _pad_lanes(w_kvb[:, :, MLA_NOPE:], LANE).reshape(MLA_KV_LORA, MLA_PAD)
    return {
        "g_mix": mix_norm[l][None],
        "w_in": w_r,
        "g_dq": dil_q_norm[l].reshape(1, DIL_WIDTH),
        "g_dk": dil_k_norm[l].reshape(1, DIL_WIDTH),
        "g_qa": mla_q_a_norm[l][None],
        "g_kva": mla_kv_a_norm[l][None],
        "w_qb": w_qb.reshape(MLA_Q_LORA, MLA_PAD).astype(BF16),
        "w_kvb": jnp.concatenate([w_kb, w_vb], axis=1).astype(BF16),
        "g_mq": _pad_lanes(mla_q_norm[l][None], LANE),
        "g_mk": _pad_lanes(mla_k_norm[l][None], LANE),
        "g_memq": jnp.tile(mem_q_norm[l][None], (1, MEM_HEADS)),
        "g_mem": mem_norm[l][None],
        "w_memkv": w_mem_kv[l].astype(BF16),
        "g_memk": mem_k_norm[l][None],
        "w_branch": w_branch[l].astype(BF16),
        "w_out": w_out[l].astype(BF16),
        "g_ffn": ffn_norm[l][None],
        "w_ff1": w_ff1[l].astype(BF16),
        "w_ff2": w_ff2[l].astype(BF16),
    }


def _encoder_layer(x, mem, p):
    batch, seq, _ = x.shape
    xf = x.reshape(batch * seq, D_MODEL)
    dq, dk, dv, qm, km, vm, mq, gl = _in_proj(xf, p, seq)
    a = _dilated(dq, dk, dv, batch, seq)
    b = _mla(qm, km, vm, batch, seq)
    kmem, vmem = _mem_kv(mem, p)
    m = _mem_attn(mq, kmem, vmem, batch, seq)
    return _post(xf, a, b, m, gl, p).reshape(batch, seq, D_MODEL)


def kernel(x_prompt, x_sample, mem_prompt, mem_sample, mix_norm, w_in, dil_q_norm, dil_k_norm, mla_q_a_norm, mla_kv_a_norm, w_mla_q_b, w_mla_kv_b, mla_q_norm, mla_k_norm, mem_norm, w_mem_kv, mem_q_norm, mem_k_norm, w_branch, w_out, ffn_norm, w_ff1, w_ff2):
    weights = (mix_norm, w_in, dil_q_norm, dil_k_norm, mla_q_a_norm, mla_kv_a_norm, w_mla_q_b,
               w_mla_kv_b, mla_q_norm, mla_k_norm, mem_norm, w_mem_kv, mem_q_norm, mem_k_norm,
               w_branch, w_out, ffn_norm, w_ff1, w_ff2)
    yp, ys = x_prompt, x_sample
    rope_p = _rope_tables(yp.shape[1])
    rope_s = _rope_tables(ys.shape[1])
    for l in range(mix_norm.shape[0]):
        p = _prep_layer(l, *weights)
        names = ("rope_c", "rope_s1", "rope_s2")
        yp = _encoder_layer(yp, mem_prompt, {**p, **dict(zip(names, rope_p))})
        ys = _encoder_layer(ys, mem_sample, {**p, **dict(zip(names, rope_s))})
    return (yp, ys)
```

```python
import functools
import math

import numpy as np
import jax
import jax.numpy as jnp
from jax import lax
from jax.experimental import pallas as pl
from jax.experimental.pallas import tpu as pltpu

F32 = jnp.float32
BF16 = jnp.bfloat16

LANE = 128
VMEM_LIMIT_BYTES = 56 * 1024 * 1024

D_MODEL = 1024
EPS = 1e-6
NEG_INF = -1e30
LOG2E = math.log2(math.e)
DIL_GROUPS = ((128, 1), (512, 4), (2048, 16))
N_DIL = 3
DIL_HEADS = 4
DIL_HEAD_DIM = 128
DIL_WIDTH = N_DIL * DIL_HEADS * DIL_HEAD_DIM
DIL_OUT = DIL_HEADS * DIL_HEAD_DIM
MLA_HEADS = 8
MLA_Q_LORA = 384
MLA_KV_LORA = 256
MLA_NOPE = 64
MLA_ROPE = 32
MLA_QK = MLA_NOPE + MLA_ROPE
MLA_V = 64
MLA_PAD = MLA_HEADS * LANE
ROPE_THETA = 10000.0
MEM_HEADS = 4
MEM_HEAD_DIM = 128
MEM_WIDTH = MEM_HEADS * MEM_HEAD_DIM
N_BRANCH = 3
BRANCH_WIDTH = 512
D_FF = 4 * D_MODEL
GATE_WIDTH = N_BRANCH * D_MODEL

COL_DQ = 0
COL_DK = COL_DQ + DIL_WIDTH
COL_DV = COL_DK + DIL_WIDTH
COL_CQ = COL_DV + DIL_WIDTH
COL_CKV = COL_CQ + MLA_Q_LORA
COL_KR = COL_CKV + MLA_KV_LORA
COL_MQ = COL_KR + LANE
COL_GL = COL_MQ + MEM_WIDTH
COL_END = COL_GL + GATE_WIDTH

DIL_R = 64
DIL_SUB = 128
DIL_TOKENS = 1024
DIL_QSCALE = LOG2E / math.sqrt(DIL_HEAD_DIM)
MLA_QSCALE = LOG2E / math.sqrt(MLA_QK)
MEM_QSCALE = LOG2E / math.sqrt(MEM_HEAD_DIM)

TOKEN_TILE = 256
MLA_TQ = 256
MLA_TK = 512
MEM_TQ = 512
FF_CHUNK = 1024

NT_DIMS = (((1,), (1,)), ((), ()))
TN_DIMS = (((0,), (0,)), ((), ()))


def _alibi_slopes():
    n = N_DIL * DIL_HEADS
    s = np.array([2.0 ** (-8.0 * (k + 1) / n) for k in range(n)], dtype=np.float32)
    return s.reshape(N_DIL, DIL_HEADS)


def _rinv(x, width):
    return lax.rsqrt(jnp.sum(x * x, axis=-1, keepdims=True) * (1.0 / width) + EPS)


def _resident(shape):
    nd = len(shape)
    return pl.BlockSpec(shape, lambda *_: (0,) * nd, pipeline_mode=pl.Buffered(1))


def _params(semantics):
    return pltpu.CompilerParams(dimension_semantics=semantics, vmem_limit_bytes=VMEM_LIMIT_BYTES)


def _rope(t, c, s1, s2):
    return t * c + pltpu.roll(t, LANE - MLA_ROPE // 2, 1) * s1 + pltpu.roll(t, MLA_ROPE // 2, 1) * s2


def _in_kernel(x_ref, gmix_ref, w_ref, gdq_ref, gdk_ref, gqa_ref, gkva_ref, wqb_ref, wkvb_ref,
               gmq_ref, gmk_ref, gmemq_ref, rc_ref, rs1_ref, rs2_ref, *refs):
    dil_refs, (qm_ref, km_ref, vm_ref, mq_ref, gl_ref) = refs[:3 * N_DIL], refs[3 * N_DIL:3 * N_DIL + 5]
    stage_refs = refs[3 * N_DIL + 5:]
    x = x_ref[...]
    h = (x * _rinv(x, D_MODEL) * gmix_ref[...]).astype(BF16)
    tm = x.shape[0]

    def proj(lo, hi):
        return jnp.dot(h, w_ref[:, lo:hi], preferred_element_type=F32)

    def head_norm(blk, gain, scale):
        return blk * _rinv(blk, LANE) * gain * scale

    def emit_dilated(z, t, gains_ref, scale):
        for g, (_, dilation) in enumerate(DIL_GROUPS):
            out_ref = dil_refs[t * N_DIL + g]
            dst = out_ref if dilation == 1 else stage_refs[t * (N_DIL - 1) + g - 1]
            for hd in range(DIL_HEADS):
                col = (g * DIL_HEADS + hd) * LANE
                blk = z[:, col:col + LANE]
                if gains_ref is not None:
                    blk = head_norm(blk, gains_ref[:, col:col + LANE], scale)
                if dilation == 1:
                    dst[:, hd * LANE:(hd + 1) * LANE] = blk.astype(BF16)
                else:
                    dst[hd] = blk
            if dilation > 1:
                for r in range(dilation):
                    for hd in range(DIL_HEADS):
                        rows = dst[hd, pl.ds(r, tm // dilation, stride=dilation), :]
                        out_ref[r, :, hd * LANE:(hd + 1) * LANE] = rows.astype(BF16)

    emit_dilated(proj(COL_DQ, COL_DK), 0, gdq_ref, DIL_QSCALE)
    emit_dilated(proj(COL_DK, COL_DV), 1, gdk_ref, 1.0)
    emit_dilated(proj(COL_DV, COL_CQ), 2, None, 1.0)
    zmq = proj(COL_MQ, COL_GL)
    for hd in range(MEM_HEADS):
        sl = slice(hd * LANE, (hd + 1) * LANE)
        mq_ref[:, sl] = head_norm(zmq[:, sl], gmemq_ref[:, sl], MEM_QSCALE).astype(BF16)
    gl_ref[...] = proj(COL_GL, COL_END).astype(BF16)

    rc, rs1, rs2 = rc_ref[...], rs1_ref[...], rs2_ref[...]

    cq = proj(COL_CQ, COL_CKV)
    cq = (cq * _rinv(cq, MLA_Q_LORA) * gqa_ref[...]).astype(BF16)
    q = jnp.dot(cq, wqb_ref[...], preferred_element_type=F32)
    ckv = proj(COL_CKV, COL_KR)
    ckv = (ckv * _rinv(ckv, MLA_KV_LORA) * gkva_ref[...]).astype(BF16)
    kv = jnp.dot(ckv, wkvb_ref[...], preferred_element_type=F32)
    k_rope = proj(COL_KR, COL_MQ)
    ones_lane = (lax.broadcasted_iota(jnp.int32, (1, LANE), 1) == MLA_V).astype(F32)
    for hd in range(MLA_HEADS):
        sl = slice(hd * LANE, (hd + 1) * LANE)
        qh = q[:, sl]
        qh = qh * _rinv(qh, MLA_QK) * gmq_ref[...]
        qm_ref[:, sl] = (_rope(qh, rc, rs1, rs2) * MLA_QSCALE).astype(BF16)
        kh = kv[:, sl] + k_rope
        kh = kh * _rinv(kh, MLA_QK) * gmk_ref[...]
        km_ref[:, sl] = _rope(kh, rc, rs1, rs2).astype(BF16)
        vm_ref[:, sl] = (kv[:, MLA_PAD + hd * LANE:MLA_PAD + (hd + 1) * LANE] + ones_lane).astype(BF16)


def _in_proj(xf, p, batch, seq):
    tokens = xf.shape[0]
    tm = TOKEN_TILE
    pos_blocks = seq // tm
    row = lambda w: pl.BlockSpec((tm, w), lambda i: (i, 0))
    rope_spec = pl.BlockSpec((tm, LANE), lambda i: (i % pos_blocks, 0))
    out_specs, out_shape, scratch = [], [], []
    for _ in range(3):
        for _, dilation in DIL_GROUPS:
            if dilation == 1:
                out_specs.append(row(DIL_OUT))
                out_shape.append(jax.ShapeDtypeStruct((tokens, DIL_OUT), BF16))
            else:
                assert tm % (dilation * 16) == 0
                out_specs.append(pl.BlockSpec((None, dilation, tm // dilation, DIL_OUT),
                                              lambda i: (i // pos_blocks, 0, i % pos_blocks, 0)))
                out_shape.append(jax.ShapeDtypeStruct((batch, dilation, seq // dilation, DIL_OUT), BF16))
                scratch.append(pltpu.VMEM((DIL_HEADS, tm, LANE), F32))
    for w in (MLA_PAD, MLA_PAD, MLA_PAD, MEM_WIDTH, GATE_WIDTH):
        out_specs.append(row(w))
        out_shape.append(jax.ShapeDtypeStruct((tokens, w), BF16))
    return pl.pallas_call(
        _in_kernel,
        grid=(tokens // tm,),
        in_specs=[row(D_MODEL), _resident((1, D_MODEL)), _resident((D_MODEL, COL_END)),
                  _resident((1, DIL_WIDTH)), _resident((1, DIL_WIDTH)),
                  _resident((1, MLA_Q_LORA)), _resident((1, MLA_KV_LORA)),
                  _resident((MLA_Q_LORA, MLA_PAD)), _resident((MLA_KV_LORA, 2 * MLA_PAD)),
                  _resident((1, LANE)), _resident((1, LANE)), _resident((1, MEM_WIDTH)),
                  rope_spec, rope_spec, rope_spec],
        out_specs=out_specs,
        out_shape=out_shape,
        scratch_shapes=scratch,
        compiler_params=_params(("parallel",)),
        name="in_proj",
    )(xf, p["g_mix"], p["w_in"], p["g_dq"], p["g_dk"], p["g_qa"], p["g_kva"], p["w_qb"], p["w_kvb"],
      p["g_mq"], p["g_mk"], p["g_memq"], p["rope_c"], p["rope_s1"], p["rope_s2"])


def _dil_kernel(*refs, slopes, dilation, sub_len, tl, sub, has_prev, last):
    q_ref, kp_ref, kc_ref, kn_ref, vp_ref, vc_ref, vn_ref = refs[:7]
    refs = refs[7:]
    if has_prev:
        oin_ref, lin_ref = refs[:2]
        refs = refs[2:]
    if last:
        o_ref, kcat, vcat = refs
    else:
        o_ref, l_ref, kcat, vcat = refs

    nkeys = sub + 2 * DIL_R
    qi = lax.broadcasted_iota(jnp.int32, (sub, nkeys), 0)
    kj = lax.broadcasted_iota(jnp.int32, (sub, nkeys), 1)
    absrel = jnp.abs(kj - DIL_R - qi)
    band = absrel <= DIL_R
    dist = absrel.astype(F32)
    block_start = pl.program_id(1) * tl

    for r in range(dilation):
        kcat[0:DIL_R, :] = kp_ref[r]
        kcat[DIL_R:DIL_R + tl, :] = kc_ref[r]
        kcat[DIL_R + tl:, :] = kn_ref[r]
        vcat[0:DIL_R, :] = vp_ref[r]
        vcat[DIL_R:DIL_R + tl, :] = vc_ref[r]
        vcat[DIL_R + tl:, :] = vn_ref[r]
        for sb in range(tl // sub):
            r0 = sb * sub
            if dilation == 1:
                rows = slice(r0, r0 + sub)
            else:
                rows = pl.ds(r0 * dilation + r, sub, stride=dilation)
            upos = kj + (block_start + (r0 - DIL_R))
            valid = band & (upos >= 0) & (upos < sub_len)
            for hd in range(DIL_HEADS):
                sl = slice(hd * LANE, (hd + 1) * LANE)
                q = q_ref[r, r0:r0 + sub, sl]
                k = kcat[r0:r0 + nkeys, sl]
                v = vcat[r0:r0 + nkeys, sl]
                s = lax.dot_general(q, k, NT_DIMS, preferred_element_type=F32)
                s = jnp.where(valid, s + dist * (-float(slopes[hd]) * dilation * LOG2E), NEG_INF)
                m = jnp.max(s, axis=-1, keepdims=True)
                pr = jnp.exp2(s - m)
                den = jnp.sum(pr, axis=-1, keepdims=True)
                o = jnp.dot(pr.astype(BF16), v, preferred_element_type=F32) / den
                lse = m + jnp.log2(den)
                if has_prev:
                    o_prev = oin_ref[hd, rows, :]
                    l_prev = lin_ref[hd, rows, :]
                    top = jnp.maximum(l_prev, lse)
                    w_prev = jnp.exp2(l_prev - top)
                    w_cur = jnp.exp2(lse - top)
                    tot = w_prev + w_cur
                    o = (w_prev * o_prev + w_cur * o) / tot
                    lse = top + jnp.log2(tot)
                o_ref[hd, rows, :] = o
                if not last:
                    l_ref[hd, rows, :] = jnp.broadcast_to(lse, (sub, LANE))


def _dilated_group(g, q, k, v, state, batch, seq):
    window, dilation = DIL_GROUPS[g]
    assert window // (2 * dilation) == DIL_R
    sub_len = seq // dilation
    assert sub_len * dilation == seq and sub_len % DIL_R == 0
    tl = min(sub_len, DIL_TOKENS // dilation)
    sub = min(tl, DIL_SUB)
    assert sub_len % tl == 0 and tl % sub == 0 and tl % DIL_R == 0
    edge_blocks = sub_len // DIL_R
    per_tl = tl // DIL_R
    has_prev = state is not None
    last = g == N_DIL - 1

    shape4 = (batch, dilation, sub_len, DIL_OUT)
    cur_spec = pl.BlockSpec((None, dilation, tl, DIL_OUT), lambda b, i: (b, 0, i, 0))
    prev_spec = pl.BlockSpec((None, dilation, DIL_R, DIL_OUT),
                             lambda b, i: (b, 0, jnp.maximum(i * per_tl - 1, 0), 0))
    next_spec = pl.BlockSpec((None, dilation, DIL_R, DIL_OUT),
                             lambda b, i: (b, 0, jnp.minimum((i + 1) * per_tl, edge_blocks - 1), 0))
    st_spec = pl.BlockSpec((None, DIL_HEADS, tl * dilation, LANE), lambda b, i: (b, 0, i, 0))
    st_shape = jax.ShapeDtypeStruct((batch, DIL_HEADS, seq, LANE), F32)

    q, k, v = (t.reshape(shape4) for t in (q, k, v))
    args = [q, k, k, k, v, v, v]
    in_specs = [cur_spec, prev_spec, cur_spec, next_spec, prev_spec, cur_spec, next_spec]
    if has_prev:
        args += list(state)
        in_specs += [st_spec, st_spec]
    n_out = 1 if last else 2
    return pl.pallas_call(
        functools.partial(_dil_kernel, slopes=_alibi_slopes()[g], dilation=dilation, sub_len=sub_len,
                          tl=tl, sub=sub, has_prev=has_prev, last=last),
        grid=(batch, sub_len // tl),
        in_specs=in_specs,
        out_specs=[st_spec] * n_out,
        out_shape=[st_shape] * n_out,
        scratch_shapes=[pltpu.VMEM((tl + 2 * DIL_R, DIL_OUT), BF16)] * 2,
        compiler_params=_params(("parallel", "parallel")),
        name=f"dilated_g{g}",
    )(*args)


def _dilated(dil_qkv, batch, seq):
    state = None
    for g in range(N_DIL):
        q, k, v = (dil_qkv[t * N_DIL + g] for t in range(3))
        state = _dilated_group(g, q, k, v, state, batch, seq)
    return state[0]


def _mla_kernel(q_ref, k_ref, v_ref, o_ref, s_buf, p_buf, *, tq, tk, nk):
    assert nk >= 2 and nk % 2 == 0
    for pair in range(MLA_HEADS // 2):
        cols = [slice(hd * LANE, (hd + 1) * LANE) for hd in (2 * pair, 2 * pair + 1)]
        qs = [q_ref[:, sl] for sl in cols]

        def tile(j):
            return pl.ds(pl.multiple_of(j * tk, tk), tk)

        def scores(j, h, slot):
            s = lax.dot_general(k_ref[tile(j), cols[h]], qs[h], NT_DIMS, preferred_element_type=F32)
            s_buf[slot, h] = s
            return jnp.max(s, axis=0, keepdims=True)

        def numerators(h, slot, m, tile_max):
            m_new = jnp.maximum(m, tile_max)
            p_buf[slot, h] = jnp.exp2(s_buf[slot, h] - m_new).astype(BF16)
            return m_new, jnp.exp2(m - m_new)

        def accumulate(j, h, slot, alpha, acc):
            pv = lax.dot_general(v_ref[tile(j), cols[h]], p_buf[slot, h], TN_DIMS,
                                 preferred_element_type=F32)
            return alpha * acc + pv

        def trip(j, slot, state):
            out = []
            for h, (m, alpha, tile_max, acc) in enumerate(state):
                acc = accumulate(j - 1, h, 1 - slot, alpha, acc)
                m, alpha = numerators(h, slot, m, tile_max)
                out.append((m, alpha, scores(j + 1, h, 1 - slot), acc))
            return tuple(out)

        state = []
        for h in range(2):
            m0 = jnp.full((1, tq), -jnp.inf, F32)
            m, alpha = numerators(h, 0, m0, scores(0, h, 0))
            state.append((m, alpha, scores(1, h, 1), jnp.zeros((LANE, tq), F32)))

        def body(jj, state):
            j = 2 * jj + 1
            return trip(j + 1, 0, trip(j, 1, state))

        state = lax.fori_loop(0, (nk - 2) // 2, body, tuple(state))
        last = (nk - 1) % 2
        tops = []
        for h, (m, alpha, tile_max, acc) in enumerate(state):
            acc = accumulate(nk - 2, h, 1 - last, alpha, acc)
            m, alpha = numerators(h, last, m, tile_max)
            acc = accumulate(nk - 1, h, last, alpha, acc)
            tops.append((acc / acc[MLA_V:MLA_V + 1, :])[:MLA_V])
        o_ref[:, pair * LANE:(pair + 1) * LANE] = jnp.concatenate(tops, axis=0).T.astype(BF16)


def _mla(qm, km, vm, batch, seq):
    tq, tk = MLA_TQ, MLA_TK
    view = lambda t: t.reshape(batch, seq, MLA_PAD)
    kv_spec = pl.BlockSpec((None, seq, MLA_PAD), lambda b, i: (b, 0, 0))
    out = pl.pallas_call(
        functools.partial(_mla_kernel, tq=tq, tk=tk, nk=seq // tk),
        grid=(batch, seq // tq),
        in_specs=[pl.BlockSpec((None, tq, MLA_PAD), lambda b, i: (b, i, 0)), kv_spec, kv_spec],
        out_specs=pl.BlockSpec((None, tq, MLA_HEADS * MLA_V), lambda b, i: (b, i, 0)),
        out_shape=jax.ShapeDtypeStruct((batch, seq, MLA_HEADS * MLA_V), BF16),
        scratch_shapes=[pltpu.VMEM((2, 2, tk, tq), F32), pltpu.VMEM((2, 2, tk, tq), BF16)],
        compiler_params=_params(("parallel", "arbitrary")),
        name="latent_attn",
    )(view(qm), view(km), view(vm))
    return out.reshape(batch * seq, MLA_HEADS * MLA_V)


def _memkv_kernel(mem_ref, g_ref, w_ref, gk_ref, k_ref, v_ref):
    x = mem_ref[...]
    h = (x * _rinv(x, D_MODEL) * g_ref[...]).astype(BF16)
    kv = jnp.dot(h, w_ref[...], preferred_element_type=F32)
    for hd in range(MEM_HEADS):
        sl = slice(hd * LANE, (hd + 1) * LANE)
        blk = kv[:, sl]
        k_ref[:, sl] = (blk * _rinv(blk, MEM_HEAD_DIM) * gk_ref[...]).astype(BF16)
    v_ref[...] = kv[:, MEM_WIDTH:].astype(BF16)


def _mem_kv(mem, p):
    batch, n_mem, _ = mem.shape
    spec = lambda w: pl.BlockSpec((None, n_mem, w), lambda b: (b, 0, 0))
    return pl.pallas_call(
        _memkv_kernel,
        grid=(batch,),
        in_specs=[spec(D_MODEL), _resident((1, D_MODEL)), _resident((D_MODEL, 2 * MEM_WIDTH)),
                  _resident((1, LANE))],
        out_specs=[spec(MEM_WIDTH), spec(MEM_WIDTH)],
        out_shape=[jax.ShapeDtypeStruct((batch, n_mem, MEM_WIDTH), BF16)] * 2,
        compiler_params=_params(("parallel",)),
        name="memory_kv",
    )(mem, p["g_mem"], p["w_memkv"], p["g_memk"])


def _memattn_kernel(q_ref, k_ref, v_ref, o_ref):
    for hd in range(MEM_HEADS):
        sl = slice(hd * LANE, (hd + 1) * LANE)
        s = lax.dot_general(q_ref[:, sl], k_ref[:, sl], NT_DIMS, preferred_element_type=F32)
        m = jnp.max(s, axis=-1, keepdims=True)
        pr = jnp.exp2(s - m)
        den = jnp.sum(pr, axis=-1, keepdims=True)
        o = jnp.dot(pr.astype(BF16), v_ref[:, sl], preferred_element_type=F32) / den
        o_ref[:, sl] = o.astype(BF16)


def _mem_attn(mq, kmem, vmem, batch, seq):
    n_mem = kmem.shape[1]
    tq = MEM_TQ
    kv_spec = pl.BlockSpec((None, n_mem, MEM_WIDTH), lambda b, i: (b, 0, 0))
    q_spec = pl.BlockSpec((None, tq, MEM_WIDTH), lambda b, i: (b, i, 0))
    out = pl.pallas_call(
        _memattn_kernel,
        grid=(batch, seq // tq),
        in_specs=[q_spec, kv_spec, kv_spec],
        out_specs=q_spec,
        out_shape=jax.ShapeDtypeStruct((batch, seq, MEM_WIDTH), BF16),
        compiler_params=_params(("parallel", "parallel")),
        name="memory_attn",
    )(mq.reshape(batch, seq, MEM_WIDTH), kmem, vmem)
    return out.reshape(batch * seq, MEM_WIDTH)


def _post_kernel(x_ref, a_ref, b_ref, m_ref, gl_ref, wb_ref, wo_ref, gffn_ref, w1_ref, w2_ref, o_ref):
    a = jnp.concatenate([a_ref[hd] for hd in range(DIL_HEADS)], axis=-1).astype(BF16)
    mixed = None
    for i, br in enumerate((a, b_ref[...], m_ref[...])):
        proj = jnp.dot(br, wb_ref[i], preferred_element_type=F32)
        gate = jax.nn.sigmoid(gl_ref[:, i * D_MODEL:(i + 1) * D_MODEL].astype(F32))
        mixed = gate * proj if mixed is None else mixed + gate * proj
    x1 = x_ref[...] + jnp.dot(mixed.astype(BF16), wo_ref[...], preferred_element_type=F32)
    h2 = (x1 * _rinv(x1, D_MODEL) * gffn_ref[...]).astype(BF16)
    acc = x1
    for c in range(D_FF // FF_CHUNK):
        sl = slice(c * FF_CHUNK, (c + 1) * FF_CHUNK)
        u = jnp.dot(h2, w1_ref[:, sl], preferred_element_type=F32)
        u = jnp.square(jnp.maximum(u, 0.0)).astype(BF16)
        acc = acc + jnp.dot(u, w2_ref[sl, :], preferred_element_type=F32)
    o_ref[...] = acc


def _post(xf, a, b, m, gl, p):
    tokens = xf.shape[0]
    tm = TOKEN_TILE
    row = lambda w: pl.BlockSpec((tm, w), lambda i: (i, 0))
    seq_blocks = a.shape[2] // tm
    a_spec = pl.BlockSpec((None, DIL_HEADS, tm, LANE), lambda i: (i // seq_blocks, 0, i % seq_blocks, 0))
    return pl.pallas_call(
        _post_kernel,
        grid=(tokens // tm,),
        in_specs=[row(D_MODEL), a_spec, row(BRANCH_WIDTH), row(BRANCH_WIDTH), row(GATE_WIDTH),
                  _resident((N_BRANCH, BRANCH_WIDTH, D_MODEL)), _resident((D_MODEL, D_MODEL)),
                  _resident((1, D_MODEL)), _resident((D_MODEL, D_FF)), _resident((D_FF, D_MODEL))],
        out_specs=row(D_MODEL),
        out_shape=jax.ShapeDtypeStruct((tokens, D_MODEL), F32),
        compiler_params=_params(("parallel",)),
        name="merge_ffn",
    )(xf, a, b, m, gl, p["w_branch"], p["w_out"], p["g_ffn"], p["w_ff1"], p["w_ff2"])


def _rope_tables(seq):
    half = MLA_ROPE // 2
    inv = ROPE_THETA ** (-jnp.arange(half, dtype=F32) * 2.0 / MLA_ROPE)
    ang = jnp.arange(seq, dtype=F32)[:, None] * inv[None, :]
    c, s = jnp.cos(ang), jnp.sin(ang)
    z = lambda w: jnp.zeros((seq, w), F32)
    tail = LANE - MLA_QK
    rope_c = jnp.concatenate([jnp.ones((seq, MLA_NOPE), F32), c, c, z(tail)], axis=-1)
    rope_s1 = jnp.concatenate([z(MLA_NOPE), -s, z(half), z(tail)], axis=-1)
    rope_s2 = jnp.concatenate([z(MLA_NOPE), z(half), s, z(tail)], axis=-1)
    return rope_c, rope_s1, rope_s2


def _pad_lanes(t, width):
    return jnp.pad(t, [(0, 0)] * (t.ndim - 1) + [(0, width - t.shape[-1])])


def _prep_layer(l, mix_norm, w_in, dil_q_norm, dil_k_norm, mla_q_a_norm, mla_kv_a_norm, w_mla_q_b,
                w_mla_kv_b, mla_q_norm, mla_k_norm, mem_norm, w_mem_kv, mem_q_norm, mem_k_norm,
                w_branch, w_out, ffn_norm, w_ff1, w_ff2):
    w = w_in[l]
    o_cq = 3 * DIL_WIDTH
    o_ckv = o_cq + MLA_Q_LORA
    o_kr = o_ckv + MLA_KV_LORA
    o_mq = o_kr + MLA_ROPE
    o_gl = o_mq + MEM_WIDTH
    kr = jnp.pad(w[:, o_kr:o_mq], ((0, 0), (MLA_NOPE, LANE - MLA_QK)))
    w_r = jnp.concatenate([w[:, :o_kr], kr, w[:, o_mq:]], axis=1).astype(BF16)
    w_qb = _pad_lanes(w_mla_q_b[l].reshape(MLA_Q_LORA, MLA_HEADS, MLA_QK), LANE)
    w_kvb = w_mla_kv_b[l].reshape(MLA_KV_LORA, MLA_HEADS, MLA_NOPE + MLA_V)
    w_kb = _pad_lanes(w_kvb[:, :, :MLA_NOPE], LANE).reshape(MLA_KV_LORA, MLA_PAD)
    w_vb = _pad_lanes(w_kvb[:, :, MLA_NOPE:], LANE).reshape(MLA_KV_LORA, MLA_PAD)
    return {
        "g_mix": mix_norm[l][None],
        "w_in": w_r,
        "g_dq": dil_q_norm[l].reshape(1, DIL_WIDTH),
        "g_dk": dil_k_norm[l].reshape(1, DIL_WIDTH),
        "g_qa": mla_q_a_norm[l][None],
        "g_kva": mla_kv_a_norm[l][None],
        "w_qb": w_qb.reshape(MLA_Q_LORA, MLA_PAD).astype(BF16),
        "w_kvb": jnp.concatenate([w_kb, w_vb], axis=1).astype(BF16),
        "g_mq": _pad_lanes(mla_q_norm[l][None], LANE),
        "g_mk": _pad_lanes(mla_k_norm[l][None], LANE),
        "g_memq": jnp.tile(mem_q_norm[l][None], (1, MEM_HEADS)),
        "g_mem": mem_norm[l][None],
        "w_memkv": w_mem_kv[l].astype(BF16),
        "g_memk": mem_k_norm[l][None],
        "w_branch": w_branch[l].astype(BF16),
        "w_out": w_out[l].astype(BF16),
        "g_ffn": ffn_norm[l][None],
        "w_ff1": w_ff1[l].astype(BF16),
        "w_ff2": w_ff2[l].astype(BF16),
    }


def _encoder_layer(x, mem, p):
    batch, seq, _ = x.shape
    xf = x.reshape(batch * seq, D_MODEL)
    outs = _in_proj(xf, p, batch, seq)
    dil_qkv, (qm, km, vm, mq, gl) = outs[:3 * N_DIL], outs[3 * N_DIL:]
    a = _dilated(dil_qkv, batch, seq)
    b = _mla(qm, km, vm, batch, seq)
    kmem, vmem = _mem_kv(mem, p)
    m = _mem_attn(mq, kmem, vmem, batch, seq)
    return _post(xf, a, b, m, gl, p).reshape(batch, seq, D_MODEL)


def kernel(x_prompt, x_sample, mem_prompt, mem_sample, mix_norm, w_in, dil_q_norm, dil_k_norm, mla_q_a_norm, mla_kv_a_norm, w_mla_q_b, w_mla_kv_b, mla_q_norm, mla_k_norm, mem_norm, w_mem_kv, mem_q_norm, mem_k_norm, w_branch, w_out, ffn_norm, w_ff1, w_ff2):
    weights = (mix_norm, w_in, dil_q_norm, dil_k_norm, mla_q_a_norm, mla_kv_a_norm, w_mla_q_b,
               w_mla_kv_b, mla_q_norm, mla_k_norm, mem_norm, w_mem_kv, mem_q_norm, mem_k_norm,
               w_branch, w_out, ffn_norm, w_ff1, w_ff2)
    yp, ys = x_prompt, x_sample
    rope_p = _rope_tables(yp.shape[1])
    rope_s = _rope_tables(ys.shape[1])
    for l in range(mix_norm.shape[0]):
        p = _prep_layer(l, *weights)
        names = ("rope_c", "rope_s1", "rope_s2")
        yp = _encoder_layer(yp, mem_prompt, {**p, **dict(zip(names, rope_p))})
        ys = _encoder_layer(ys, mem_sample, {**p, **dict(zip(names, rope_s))})
    return (yp, ys)
```

```python
import functools
import math

import numpy as np
import jax
import jax.numpy as jnp
from jax import lax
from jax.experimental import pallas as pl
from jax.experimental.pallas import tpu as pltpu

F32 = jnp.float32
BF16 = jnp.bfloat16

LANE = 128
VMEM_LIMIT_BYTES = 56 * 1024 * 1024

D_MODEL = 1024
EPS = 1e-6
NEG_INF = -1e30
LOG2E = math.log2(math.e)
DIL_GROUPS = ((128, 1), (512, 4), (2048, 16))
N_DIL = 3
DIL_HEADS = 4
DIL_HEAD_DIM = 128
DIL_WIDTH = N_DIL * DIL_HEADS * DIL_HEAD_DIM
DIL_OUT = DIL_HEADS * DIL_HEAD_DIM
MLA_HEADS = 8
MLA_Q_LORA = 384
MLA_KV_LORA = 256
MLA_NOPE = 64
MLA_ROPE = 32
MLA_QK = MLA_NOPE + MLA_ROPE
MLA_V = 64
MLA_PAD = MLA_HEADS * LANE
ROPE_THETA = 10000.0
MEM_HEADS = 4
MEM_HEAD_DIM = 128
MEM_WIDTH = MEM_HEADS * MEM_HEAD_DIM
N_BRANCH = 3
BRANCH_WIDTH = 512
D_FF = 4 * D_MODEL
GATE_WIDTH = N_BRANCH * D_MODEL

COL_DQ = 0
COL_DK = COL_DQ + DIL_WIDTH
COL_DV = COL_DK + DIL_WIDTH
COL_CQ = COL_DV + DIL_WIDTH
COL_CKV = COL_CQ + MLA_Q_LORA
COL_KR = COL_CKV + MLA_KV_LORA
COL_MQ = COL_KR + LANE
COL_GL = COL_MQ + MEM_WIDTH
COL_END = COL_GL + GATE_WIDTH

DIL_R = 64
DIL_SUB = 128
DIL_TOKENS = 1024
DIL_BATCH_ROWS = 256
DIL_QSCALE = LOG2E / math.sqrt(DIL_HEAD_DIM)
MLA_QSCALE = LOG2E / math.sqrt(MLA_QK)
MEM_QSCALE = LOG2E / math.sqrt(MEM_HEAD_DIM)

TOKEN_TILE = 256
MLA_TQ = 256
MLA_TK = 512
MLA_GROUP = 8
MEM_TQ = 512
FF_CHUNK = 1024

NT_DIMS = (((1,), (1,)), ((), ()))
TN_DIMS = (((0,), (0,)), ((), ()))


def _alibi_slopes():
    n = N_DIL * DIL_HEADS
    s = np.array([2.0 ** (-8.0 * (k + 1) / n) for k in range(n)], dtype=np.float32)
    return s.reshape(N_DIL, DIL_HEADS)


def _rinv(x, width):
    return lax.rsqrt(jnp.sum(x * x, axis=-1, keepdims=True) * (1.0 / width) + EPS)


def _resident(shape):
    nd = len(shape)
    return pl.BlockSpec(shape, lambda *_: (0,) * nd, pipeline_mode=pl.Buffered(1))


def _params(semantics):
    return pltpu.CompilerParams(dimension_semantics=semantics, vmem_limit_bytes=VMEM_LIMIT_BYTES)


def _rope(t, c, s1, s2):
    return t * c + pltpu.roll(t, LANE - MLA_ROPE // 2, 1) * s1 + pltpu.roll(t, MLA_ROPE // 2, 1) * s2


def _in_kernel(x_ref, gmix_ref, w_ref, gdq_ref, gdk_ref, gqa_ref, gkva_ref, wqb_ref, wkvb_ref,
               gmq_ref, gmk_ref, gmemq_ref, rc_ref, rs1_ref, rs2_ref, *refs):
    dil_refs, (qm_ref, km_ref, vm_ref, mq_ref, gl_ref) = refs[:3 * N_DIL], refs[3 * N_DIL:3 * N_DIL + 5]
    stage_refs = refs[3 * N_DIL + 5:]
    x = x_ref[...]
    h = (x * _rinv(x, D_MODEL) * gmix_ref[...]).astype(BF16)
    tm = x.shape[0]

    def proj(lo, hi):
        return jnp.dot(h, w_ref[:, lo:hi], preferred_element_type=F32)

    def head_norm(blk, gain, scale):
        return blk * _rinv(blk, LANE) * gain * scale

    def emit_dilated(z, t, gains_ref, scale):
        for g, (_, dilation) in enumerate(DIL_GROUPS):
            out_ref = dil_refs[t * N_DIL + g]
            dst = out_ref if dilation == 1 else stage_refs[t * (N_DIL - 1) + g - 1]
            for hd in range(DIL_HEADS):
                col = (g * DIL_HEADS + hd) * LANE
                blk = z[:, col:col + LANE]
                if gains_ref is not None:
                    blk = head_norm(blk, gains_ref[:, col:col + LANE], scale)
                if dilation == 1:
                    dst[:, hd * LANE:(hd + 1) * LANE] = blk.astype(BF16)
                else:
                    dst[hd] = blk
            if dilation > 1:
                for r in range(dilation):
                    for hd in range(DIL_HEADS):
                        rows = dst[hd, pl.ds(r, tm // dilation, stride=dilation), :]
                        out_ref[r, :, hd * LANE:(hd + 1) * LANE] = rows.astype(BF16)

    emit_dilated(proj(COL_DQ, COL_DK), 0, gdq_ref, DIL_QSCALE)
    emit_dilated(proj(COL_DK, COL_DV), 1, gdk_ref, 1.0)
    emit_dilated(proj(COL_DV, COL_CQ), 2, None, 1.0)
    zmq = proj(COL_MQ, COL_GL)
    for hd in range(MEM_HEADS):
        sl = slice(hd * LANE, (hd + 1) * LANE)
        mq_ref[:, sl] = head_norm(zmq[:, sl], gmemq_ref[:, sl], MEM_QSCALE).astype(BF16)
    gl_ref[...] = proj(COL_GL, COL_END).astype(BF16)

    rc, rs1, rs2 = rc_ref[...], rs1_ref[...], rs2_ref[...]

    cq = proj(COL_CQ, COL_CKV)
    cq = (cq * _rinv(cq, MLA_Q_LORA) * gqa_ref[...]).astype(BF16)
    q = jnp.dot(cq, wqb_ref[...], preferred_element_type=F32)
    ckv = proj(COL_CKV, COL_KR)
    ckv = (ckv * _rinv(ckv, MLA_KV_LORA) * gkva_ref[...]).astype(BF16)
    kv = jnp.dot(ckv, wkvb_ref[...], preferred_element_type=F32)
    k_rope = proj(COL_KR, COL_MQ)
    ones_lane = (lax.broadcasted_iota(jnp.int32, (1, LANE), 1) == MLA_V).astype(F32)
    for hd in range(MLA_HEADS):
        sl = slice(hd * LANE, (hd + 1) * LANE)
        qh = q[:, sl]
        qh = qh * _rinv(qh, MLA_QK) * gmq_ref[...]
        qm_ref[:, sl] = (_rope(qh, rc, rs1, rs2) * MLA_QSCALE).astype(BF16)
        kh = kv[:, sl] + k_rope
        kh = kh * _rinv(kh, MLA_QK) * gmk_ref[...]
        km_ref[:, sl] = _rope(kh, rc, rs1, rs2).astype(BF16)
        vm_ref[:, sl] = (kv[:, MLA_PAD + hd * LANE:MLA_PAD + (hd + 1) * LANE] + ones_lane).astype(BF16)


def _in_proj(xf, p, batch, seq):
    tokens = xf.shape[0]
    tm = TOKEN_TILE
    pos_blocks = seq // tm
    row = lambda w: pl.BlockSpec((tm, w), lambda i: (i, 0))
    rope_spec = pl.BlockSpec((tm, LANE), lambda i: (i % pos_blocks, 0))
    out_specs, out_shape, scratch = [], [], []
    for _ in range(3):
        for _, dilation in DIL_GROUPS:
            if dilation == 1:
                out_specs.append(row(DIL_OUT))
                out_shape.append(jax.ShapeDtypeStruct((tokens, DIL_OUT), BF16))
            else:
                assert tm % (dilation * 16) == 0
                out_specs.append(pl.BlockSpec((None, dilation, tm // dilation, DIL_OUT),
                                              lambda i: (i // pos_blocks, 0, i % pos_blocks, 0)))
                out_shape.append(jax.ShapeDtypeStruct((batch, dilation, seq // dilation, DIL_OUT), BF16))
                scratch.append(pltpu.VMEM((DIL_HEADS, tm, LANE), F32))
    for w in (MLA_PAD, MLA_PAD, MLA_PAD, MEM_WIDTH, GATE_WIDTH):
        out_specs.append(row(w))
        out_shape.append(jax.ShapeDtypeStruct((tokens, w), BF16))
    return pl.pallas_call(
        _in_kernel,
        grid=(tokens // tm,),
        in_specs=[row(D_MODEL), _resident((1, D_MODEL)), _resident((D_MODEL, COL_END)),
                  _resident((1, DIL_WIDTH)), _resident((1, DIL_WIDTH)),
                  _resident((1, MLA_Q_LORA)), _resident((1, MLA_KV_LORA)),
                  _resident((MLA_Q_LORA, MLA_PAD)), _resident((MLA_KV_LORA, 2 * MLA_PAD)),
                  _resident((1, LANE)), _resident((1, LANE)), _resident((1, MEM_WIDTH)),
                  rope_spec, rope_spec, rope_spec],
        out_specs=out_specs,
        out_shape=out_shape,
        scratch_shapes=scratch,
        compiler_params=_params(("parallel",)),
        name="in_proj",
    )(xf, p["g_mix"], p["w_in"], p["g_dq"], p["g_dk"], p["g_qa"], p["g_kva"], p["w_qb"], p["w_kvb"],
      p["g_mq"], p["g_mk"], p["g_memq"], p["rope_c"], p["rope_s1"], p["rope_s2"])


def _dil_kernel(*refs, slopes, dilation, sub_len, tl, sub, has_prev, last):
    q_ref, kp_ref, kc_ref, kn_ref, vp_ref, vc_ref, vn_ref = refs[:7]
    refs = refs[7:]
    if has_prev:
        oin_ref, lin_ref = refs[:2]
        refs = refs[2:]
    if last:
        o_ref, kcat, vcat = refs
    else:
        o_ref, l_ref, kcat, vcat = refs

    for r in range(dilation):
        kcat[r, 0:DIL_R, :] = kp_ref[r]
        kcat[r, DIL_R:DIL_R + tl, :] = kc_ref[r]
        kcat[r, DIL_R + tl:, :] = kn_ref[r]
        vcat[r, 0:DIL_R, :] = vp_ref[r]
        vcat[r, DIL_R:DIL_R + tl, :] = vc_ref[r]
        vcat[r, DIL_R + tl:, :] = vn_ref[r]

    nkeys = sub + 2 * DIL_R
    qi = lax.broadcasted_iota(jnp.int32, (sub, nkeys), 0)
    kj = lax.broadcasted_iota(jnp.int32, (sub, nkeys), 1)
    absrel = jnp.abs(kj - DIL_R - qi)
    band = absrel <= DIL_R
    dist = absrel.astype(F32)
    block_start = pl.program_id(1) * tl
    valid = []
    for sb in range(tl // sub):
        upos = kj + (block_start + (sb * sub - DIL_R))
        valid.append(band & (upos >= 0) & (upos < sub_len))

    units = [(r, sb) for r in range(dilation) for sb in range(tl // sub)]
    batch = max(1, DIL_BATCH_ROWS // sub)
    for u0 in range(0, len(units), batch):
        blocks = [(r, sb, hd) for r, sb in units[u0:u0 + batch] for hd in range(DIL_HEADS)]
        cols = lambda hd: slice(hd * LANE, (hd + 1) * LANE)
        scores = []
        for r, sb, hd in blocks:
            r0 = sb * sub
            s = lax.dot_general(q_ref[r, r0:r0 + sub, cols(hd)], kcat[r, r0:r0 + nkeys, cols(hd)],
                                NT_DIMS, preferred_element_type=F32)
            bias = dist * (-float(slopes[hd]) * dilation * LOG2E)
            scores.append(jnp.where(valid[sb], s + bias, NEG_INF))
        stats = []
        for s in scores:
            m = jnp.max(s, axis=-1, keepdims=True)
            pr = jnp.exp2(s - m)
            stats.append((m, pr.astype(BF16), jnp.sum(pr, axis=-1, keepdims=True)))
        outs = []
        for (r, sb, hd), (m, pr, den) in zip(blocks, stats):
            r0 = sb * sub
            o = jnp.dot(pr, vcat[r, r0:r0 + nkeys, cols(hd)], preferred_element_type=F32) / den
            outs.append((o, m + jnp.log2(den)))
        for (r, sb, hd), (o, lse) in zip(blocks, outs):
            r0 = sb * sub
            if dilation == 1:
                rows = slice(r0, r0 + sub)
            else:
                rows = pl.ds(r0 * dilation + r, sub, stride=dilation)
            if has_prev:
                o_prev = oin_ref[hd, rows, :]
                l_prev = lin_ref[hd, rows, :]
                top = jnp.maximum(l_prev, lse)
                w_prev = jnp.exp2(l_prev - top)
                w_cur = jnp.exp2(lse - top)
                tot = w_prev + w_cur
                o = (w_prev * o_prev + w_cur * o) / tot
                lse = top + jnp.log2(tot)
            o_ref[hd, rows, :] = o
            if not last:
                l_ref[hd, rows, :] = jnp.broadcast_to(lse, (sub, LANE))


def _dilated_group(g, q, k, v, state, batch, seq):
    window, dilation = DIL_GROUPS[g]
    assert window // (2 * dilation) == DIL_R
    sub_len = seq // dilation
    assert sub_len * dilation == seq and sub_len % DIL_R == 0
    tl = min(sub_len, DIL_TOKENS // dilation)
    sub = min(tl, DIL_SUB)
    assert sub_len % tl == 0 and tl % sub == 0 and tl % DIL_R == 0
    edge_blocks = sub_len // DIL_R
    per_tl = tl // DIL_R
    has_prev = state is not None
    last = g == N_DIL - 1

    shape4 = (batch, dilation, sub_len, DIL_OUT)
    cur_spec = pl.BlockSpec((None, dilation, tl, DIL_OUT), lambda b, i: (b, 0, i, 0))
    prev_spec = pl.BlockSpec((None, dilation, DIL_R, DIL_OUT),
                             lambda b, i: (b, 0, jnp.maximum(i * per_tl - 1, 0), 0))
    next_spec = pl.BlockSpec((None, dilation, DIL_R, DIL_OUT),
                             lambda b, i: (b, 0, jnp.minimum((i + 1) * per_tl, edge_blocks - 1), 0))
    st_spec = pl.BlockSpec((None, DIL_HEADS, tl * dilation, LANE), lambda b, i: (b, 0, i, 0))
    st_shape = jax.ShapeDtypeStruct((batch, DIL_HEADS, seq, LANE), F32)

    q, k, v = (t.reshape(shape4) for t in (q, k, v))
    args = [q, k, k, k, v, v, v]
    in_specs = [cur_spec, prev_spec, cur_spec, next_spec, prev_spec, cur_spec, next_spec]
    if has_prev:
        args += list(state)
        in_specs += [st_spec, st_spec]
    n_out = 1 if last else 2
    return pl.pallas_call(
        functools.partial(_dil_kernel, slopes=_alibi_slopes()[g], dilation=dilation, sub_len=sub_len,
                          tl=tl, sub=sub, has_prev=has_prev, last=last),
        grid=(batch, sub_len // tl),
        in_specs=in_specs,
        out_specs=[st_spec] * n_out,
        out_shape=[st_shape] * n_out,
        scratch_shapes=[pltpu.VMEM((dilation, tl + 2 * DIL_R, DIL_OUT), BF16)] * 2,
        compiler_params=_params(("parallel", "parallel")),
        name=f"dilated_g{g}",
    )(*args)


def _dilated(dil_qkv, batch, seq):
    state = None
    for g in range(N_DIL):
        q, k, v = (dil_qkv[t * N_DIL + g] for t in range(3))
        state = _dilated_group(g, q, k, v, state, batch, seq)
    return state[0]


def _mla_kernel(q_ref, k_ref, v_ref, o_ref, s_buf, p_buf, *, tq, tk, nk):
    assert nk >= 2 and nk % 2 == 0
    for grp in range(MLA_HEADS // MLA_GROUP):
        cols = [slice(hd * LANE, (hd + 1) * LANE) for hd in range(grp * MLA_GROUP, (grp + 1) * MLA_GROUP)]
        qs = [q_ref[:, sl] for sl in cols]

        def tile(j):
            return pl.ds(pl.multiple_of(j * tk, tk), tk)

        def scores(j, h, slot):
            s = lax.dot_general(k_ref[tile(j), cols[h]], qs[h], NT_DIMS, preferred_element_type=F32)
            s_buf[slot, h] = s
            return jnp.max(s, axis=0, keepdims=True)

        def numerators(h, slot, m, tile_max):
            m_new = jnp.maximum(m, tile_max)
            p_buf[slot, h] = jnp.exp2(s_buf[slot, h] - m_new).astype(BF16)
            return m_new, jnp.exp2(m - m_new)

        def accumulate(j, h, slot, alpha, acc):
            pv = lax.dot_general(v_ref[tile(j), cols[h]], p_buf[slot, h], TN_DIMS,
                                 preferred_element_type=F32)
            return alpha * acc + pv

        def trip(j, slot, state):
            out = []
            for h, (m, alpha, tile_max, acc) in enumerate(state):
                acc = accumulate(j - 1, h, 1 - slot, alpha, acc)
                m, alpha = numerators(h, slot, m, tile_max)
                out.append((m, alpha, scores(j + 1, h, 1 - slot), acc))
            return tuple(out)

        state = []
        for h in range(MLA_GROUP):
            m0 = jnp.full((1, tq), -jnp.inf, F32)
            m, alpha = numerators(h, 0, m0, scores(0, h, 0))
            state.append((m, alpha, scores(1, h, 1), jnp.zeros((LANE, tq), F32)))

        def body(jj, state):
            j = 2 * jj + 1
            return trip(j + 1, 0, trip(j, 1, state))

        state = lax.fori_loop(0, (nk - 2) // 2, body, tuple(state))
        last = (nk - 1) % 2
        tops = []
        for h, (m, alpha, tile_max, acc) in enumerate(state):
            acc = accumulate(nk - 2, h, 1 - last, alpha, acc)
            m, alpha = numerators(h, last, m, tile_max)
            acc = accumulate(nk - 1, h, last, alpha, acc)
            tops.append((acc / acc[MLA_V:MLA_V + 1, :])[:MLA_V])
        for i in range(MLA_GROUP // 2):
            col = (grp * MLA_GROUP // 2 + i) * LANE
            o_ref[:, col:col + LANE] = jnp.concatenate(tops[2 * i:2 * i + 2], axis=0).T.astype(BF16)


def _mla(qm, km, vm, batch, seq):
    tq, tk = MLA_TQ, MLA_TK
    view = lambda t: t.reshape(batch, seq, MLA_PAD)
    kv_spec = pl.BlockSpec((None, seq, MLA_PAD), lambda b, i: (b, 0, 0))
    out = pl.pallas_call(
        functools.partial(_mla_kernel, tq=tq, tk=tk, nk=seq // tk),
        grid=(batch, seq // tq),
        in_specs=[pl.BlockSpec((None, tq, MLA_PAD), lambda b, i: (b, i, 0)), kv_spec, kv_spec],
        out_specs=pl.BlockSpec((None, tq, MLA_HEADS * MLA_V), lambda b, i: (b, i, 0)),
        out_shape=jax.ShapeDtypeStruct((batch, seq, MLA_HEADS * MLA_V), BF16),
        scratch_shapes=[pltpu.VMEM((2, MLA_GROUP, tk, tq), F32), pltpu.VMEM((2, MLA_GROUP, tk, tq), BF16)],
        compiler_params=_params(("parallel", "arbitrary")),
        name="latent_attn",
    )(view(qm), view(km), view(vm))
    return out.reshape(batch * seq, MLA_HEADS * MLA_V)


def _memkv_kernel(mem_ref, g_ref, w_ref, gk_ref, k_ref, v_ref):
    x = mem_ref[...]
    h = (x * _rinv(x, D_MODEL) * g_ref[...]).astype(BF16)
    kv = jnp.dot(h, w_ref[...], preferred_element_type=F32)
    for hd in range(MEM_HEADS):
        sl = slice(hd * LANE, (hd + 1) * LANE)
        blk = kv[:, sl]
        k_ref[:, sl] = (blk * _rinv(blk, MEM_HEAD_DIM) * gk_ref[...]).astype(BF16)
    v_ref[...] = kv[:, MEM_WIDTH:].astype(BF16)


def _mem_kv(mem, p):
    batch, n_mem, _ = mem.shape
    spec = lambda w: pl.BlockSpec((None, n_mem, w), lambda b: (b, 0, 0))
    return pl.pallas_call(
        _memkv_kernel,
        grid=(batch,),
        in_specs=[spec(D_MODEL), _resident((1, D_MODEL)), _resident((D_MODEL, 2 * MEM_WIDTH)),
                  _resident((1, LANE))],
        out_specs=[spec(MEM_WIDTH), spec(MEM_WIDTH)],
        out_shape=[jax.ShapeDtypeStruct((batch, n_mem, MEM_WIDTH), BF16)] * 2,
        compiler_params=_params(("parallel",)),
        name="memory_kv",
    )(mem, p["g_mem"], p["w_memkv"], p["g_memk"])


def _memattn_kernel(q_ref, k_ref, v_ref, o_ref):
    for hd in range(MEM_HEADS):
        sl = slice(hd * LANE, (hd + 1) * LANE)
        s = lax.dot_general(q_ref[:, sl], k_ref[:, sl], NT_DIMS, preferred_element_type=F32)
        m = jnp.max(s, axis=-1, keepdims=True)
        pr = jnp.exp2(s - m)
        den = jnp.sum(pr, axis=-1, keepdims=True)
        o = jnp.dot(pr.astype(BF16), v_ref[:, sl], preferred_element_type=F32) / den
        o_ref[:, sl] = o.astype(BF16)


def _mem_attn(mq, kmem, vmem, batch, seq):
    n_mem = kmem.shape[1]
    tq = MEM_TQ
    kv_spec = pl.BlockSpec((None, n_mem, MEM_WIDTH), lambda b, i: (b, 0, 0))
    q_spec = pl.BlockSpec((None, tq, MEM_WIDTH), lambda b, i: (b, i, 0))
    out = pl.pallas_call(
        _memattn_kernel,
        grid=(batch, seq // tq),
        in_specs=[q_spec, kv_spec, kv_spec],
        out_specs=q_spec,
        out_shape=jax.ShapeDtypeStruct((batch, seq, MEM_WIDTH), BF16),
        compiler_params=_params(("parallel", "parallel")),
        name="memory_attn",
    )(mq.reshape(batch, seq, MEM_WIDTH), kmem, vmem)
    return out.reshape(batch * seq, MEM_WIDTH)


def _post_kernel(x_ref, a_ref, b_ref, m_ref, gl_ref, wb_ref, wo_ref, gffn_ref, w1_ref, w2_ref, o_ref):
    a = jnp.concatenate([a_ref[hd] for hd in range(DIL_HEADS)], axis=-1).astype(BF16)
    mixed = None
    for i, br in enumerate((a, b_ref[...], m_ref[...])):
        proj = jnp.dot(br, wb_ref[i], preferred_element_type=F32)
        gate = jax.nn.sigmoid(gl_ref[:, i * D_MODEL:(i + 1) * D_MODEL].astype(F32))
        mixed = gate * proj if mixed is None else mixed + gate * proj
    x1 = x_ref[...] + jnp.dot(mixed.astype(BF16), wo_ref[...], preferred_element_type=F32)
    h2 = (x1 * _rinv(x1, D_MODEL) * gffn_ref[...]).astype(BF16)
    acc = x1
    for c in range(D_FF // FF_CHUNK):
        sl = slice(c * FF_CHUNK, (c + 1) * FF_CHUNK)
        u = jnp.dot(h2, w1_ref[:, sl], preferred_element_type=F32)
        u = jnp.square(jnp.maximum(u, 0.0)).astype(BF16)
        acc = acc + jnp.dot(u, w2_ref[sl, :], preferred_element_type=F32)
    o_ref[...] = acc


def _post(xf, a, b, m, gl, p):
    tokens = xf.shape[0]
    tm = TOKEN_TILE
    row = lambda w: pl.BlockSpec((tm, w), lambda i: (i, 0))
    seq_blocks = a.shape[2] // tm
    a_spec = pl.BlockSpec((None, DIL_HEADS, tm, LANE), lambda i: (i // seq_blocks, 0, i % seq_blocks, 0))
    return pl.pallas_call(
        _post_kernel,
        grid=(tokens // tm,),
        in_specs=[row(D_MODEL), a_spec, row(BRANCH_WIDTH), row(BRANCH_WIDTH), row(GATE_WIDTH),
                  _resident((N_BRANCH, BRANCH_WIDTH, D_MODEL)), _resident((D_MODEL, D_MODEL)),
                  _resident((1, D_MODEL)), _resident((D_MODEL, D_FF)), _resident((D_FF, D_MODEL))],
        out_specs=row(D_MODEL),
        out_shape=jax.ShapeDtypeStruct((tokens, D_MODEL), F32),
        compiler_params=_params(("parallel",)),
        name="merge_ffn",
    )(xf, a, b, m, gl, p["w_branch"], p["w_out"], p["g_ffn"], p["w_ff1"], p["w_ff2"])


def _rope_tables(seq):
    half = MLA_ROPE // 2
    inv = ROPE_THETA ** (-jnp.arange(half, dtype=F32) * 2.0 / MLA_ROPE)
    ang = jnp.arange(seq, dtype=F32)[:, None] * inv[None, :]
    c, s = jnp.cos(ang), jnp.sin(ang)
    z = lambda w: jnp.zeros((seq, w), F32)
    tail = LANE - MLA_QK
    rope_c = jnp.concatenate([jnp.ones((seq, MLA_NOPE), F32), c, c, z(tail)], axis=-1)
    rope_s1 = jnp.concatenate([z(MLA_NOPE), -s, z(half), z(tail)], axis=-1)
    rope_s2 = jnp.concatenate([z(MLA_NOPE), z(half), s, z(tail)], axis=-1)
    return rope_c, rope_s1, rope_s2


def _pad_lanes(t, width):
    return jnp.pad(t, [(0, 0)] * (t.ndim - 1) + [(0, width - t.shape[-1])])


def _prep_layer(l, mix_norm, w_in, dil_q_norm, dil_k_norm, mla_q_a_norm, mla_kv_a_norm, w_mla_q_b,
                w_mla_kv_b, mla_q_norm, mla_k_norm, mem_norm, w_mem_kv, mem_q_norm, mem_k_norm,
                w_branch, w_out, ffn_norm, w_ff1, w_ff2):
    w = w_in[l]
    o_cq = 3 * DIL_WIDTH
    o_ckv = o_cq + MLA_Q_LORA
    o_kr = o_ckv + MLA_KV_LORA
    o_mq = o_kr + MLA_ROPE
    o_gl = o_mq + MEM_WIDTH
    kr = jnp.pad(w[:, o_kr:o_mq], ((0, 0), (MLA_NOPE, LANE - MLA_QK)))
    w_r = jnp.concatenate([w[:, :o_kr], kr, w[:, o_mq:]], axis=1).astype(BF16)
    w_qb = _pad_lanes(w_mla_q_b[l].reshape(MLA_Q_LORA, MLA_HEADS, MLA_QK), LANE)
    w_kvb = w_mla_kv_b[l].reshape(MLA_KV_LORA, MLA_HEADS, MLA_NOPE + MLA_V)
    w_kb = _pad_lanes(w_kvb[:, :, :MLA_NOPE], LANE).reshape(MLA_KV_LORA, MLA_PAD)
    w_vb = _pad_lanes(w_kvb[:, :, MLA_NOPE:], LANE).reshape(MLA_KV_LORA, MLA_PAD)
    return {
        "g_mix": mix_norm[l][None],
        "w_in": w_r,
        "g_dq": dil_q_norm[l].reshape(1, DIL_WIDTH),
        "g_dk": dil_k_norm[l].reshape(1, DIL_WIDTH),
        "g_qa": mla_q_a_norm[l][None],
        "g_kva": mla_kv_a_norm[l][None],
        "w_qb": w_qb.reshape(MLA_Q_LORA, MLA_PAD).astype(BF16),
        "w_kvb": jnp.concatenate([w_kb, w_vb], axis=1).astype(BF16),
        "g_mq": _pad_lanes(mla_q_norm[l][None], LANE),
        "g_mk": _pad_lanes(mla_k_norm[l][None], LANE),
        "g_memq": jnp.tile(mem_q_norm[l][None], (1, MEM_HEADS)),
        "g_mem": mem_norm[l][None],
        "w_memkv": w_mem_kv[l].astype(BF16),
        "g_memk": mem_k_norm[l][None],
        "w_branch": w_branch[l].astype(BF16),
        "w_out": w_out[l].astype(BF16),
        "g_ffn": ffn_norm[l][None],
        "w_ff1": w_ff1[l].astype(BF16),
        "w_ff2": w_ff2[l].astype(BF16),
    }


def _encoder_layer(x, mem, p):
    batch, seq, _ = x.shape
    xf = x.reshape(batch * seq, D_MODEL)
    outs = _in_proj(xf, p, batch, seq)
    dil_qkv, (qm, km, vm, mq, gl) = outs[:3 * N_DIL], outs[3 * N_DIL:]
    a = _dilated(dil_qkv, batch, seq)
    b = _mla(qm, km, vm, batch, seq)
    kmem, vmem = _mem_kv(mem, p)
    m = _mem_attn(mq, kmem, vmem, batch, seq)
    return _post(xf, a, b, m, gl, p).reshape(batch, seq, D_MODEL)


def kernel(x_prompt, x_sample, mem_prompt, mem_sample, mix_norm, w_in, dil_q_norm, dil_k_norm, mla_q_a_norm, mla_kv_a_norm, w_mla_q_b, w_mla_kv_b, mla_q_norm, mla_k_norm, mem_norm, w_mem_kv, mem_q_norm, mem_k_norm, w_branch, w_out, ffn_norm, w_ff1, w_ff2):
    weights = (mix_norm, w_in, dil_q_norm, dil_k_norm, mla_q_a_norm, mla_kv_a_norm, w_mla_q_b,
               w_mla_kv_b, mla_q_norm, mla_k_norm, mem_norm, w_mem_kv, mem_q_norm, mem_k_norm,
               w_branch, w_out, ffn_norm, w_ff1, w_ff2)
    yp, ys = x_prompt, x_sample
    rope_p = _rope_tables(yp.shape[1])
    rope_s = _rope_tables(ys.shape[1])
    for l in range(mix_norm.shape[0]):
        p = _prep_layer(l, *weights)
        names = ("rope_c", "rope_s1", "rope_s2")
        yp = _encoder_layer(yp, mem_prompt, {**p, **dict(zip(names, rope_p))})
        ys = _encoder_layer(ys, mem_sample, {**p, **dict(zip(names, rope_s))})
    return (yp, ys)
```

```python
import functools
import math

import numpy as np
import jax
import jax.numpy as jnp
from jax import lax
from jax.experimental import pallas as pl
from jax.experimental.pallas import tpu as pltpu

F32 = jnp.float32
BF16 = jnp.bfloat16

LANE = 128
VMEM_LIMIT_BYTES = 56 * 1024 * 1024

D_MODEL = 1024
EPS = 1e-6
NEG_INF = -1e30
LOG2E = math.log2(math.e)
DIL_GROUPS = ((128, 1), (512, 4), (2048, 16))
N_DIL = 3
DIL_HEADS = 4
DIL_HEAD_DIM = 128
DIL_WIDTH = N_DIL * DIL_HEADS * DIL_HEAD_DIM
DIL_OUT = DIL_HEADS * DIL_HEAD_DIM
MLA_HEADS = 8
MLA_Q_LORA = 384
MLA_KV_LORA = 256
MLA_NOPE = 64
MLA_ROPE = 32
MLA_QK = MLA_NOPE + MLA_ROPE
MLA_V = 64
MLA_PAD = MLA_HEADS * LANE
ROPE_THETA = 10000.0
MEM_HEADS = 4
MEM_HEAD_DIM = 128
MEM_WIDTH = MEM_HEADS * MEM_HEAD_DIM
N_BRANCH = 3
BRANCH_WIDTH = 512
D_FF = 4 * D_MODEL
GATE_WIDTH = N_BRANCH * D_MODEL

COL_DQ = 0
COL_DK = COL_DQ + DIL_WIDTH
COL_DV = COL_DK + DIL_WIDTH
COL_CQ = COL_DV + DIL_WIDTH
COL_CKV = COL_CQ + MLA_Q_LORA
COL_KR = COL_CKV + MLA_KV_LORA
COL_MQ = COL_KR + LANE
COL_GL = COL_MQ + MEM_WIDTH
COL_END = COL_GL + GATE_WIDTH

DIL_R = 64
DIL_SUB = 128
DIL_TOKENS = 1024
DIL_BATCH_ROWS = 256
DIL_QSCALE = LOG2E / math.sqrt(DIL_HEAD_DIM)
MLA_QSCALE = LOG2E / math.sqrt(MLA_QK)
MEM_QSCALE = LOG2E / math.sqrt(MEM_HEAD_DIM)

TOKEN_TILE = 256
MLA_TQ = 256
MLA_TK = 512
MLA_GROUP = 8
MEM_TQ = 512
FF_CHUNK = 1024

NT_DIMS = (((1,), (1,)), ((), ()))
TN_DIMS = (((0,), (0,)), ((), ()))


def _alibi_slopes():
    n = N_DIL * DIL_HEADS
    s = np.array([2.0 ** (-8.0 * (k + 1) / n) for k in range(n)], dtype=np.float32)
    return s.reshape(N_DIL, DIL_HEADS)


def _rinv(x, width):
    return lax.rsqrt(jnp.sum(x * x, axis=-1, keepdims=True) * (1.0 / width) + EPS)


def _resident(shape):
    nd = len(shape)
    return pl.BlockSpec(shape, lambda *_: (0,) * nd, pipeline_mode=pl.Buffered(1))


def _params(semantics):
    return pltpu.CompilerParams(dimension_semantics=semantics, vmem_limit_bytes=VMEM_LIMIT_BYTES)


def _rope(t, c, s1, s2):
    return t * c + pltpu.roll(t, LANE - MLA_ROPE // 2, 1) * s1 + pltpu.roll(t, MLA_ROPE // 2, 1) * s2


def _in_kernel(x_ref, gmix_ref, w_ref, gdq_ref, gdk_ref, gqa_ref, gkva_ref, wqb_ref, wkvb_ref,
               gmq_ref, gmk_ref, gmemq_ref, rc_ref, rs1_ref, rs2_ref, *refs):
    dil_refs, (qm_ref, km_ref, vm_ref, mq_ref, gl_ref) = refs[:3 * N_DIL], refs[3 * N_DIL:3 * N_DIL + 5]
    stage_refs = refs[3 * N_DIL + 5:]
    x = x_ref[...]
    h = (x * _rinv(x, D_MODEL) * gmix_ref[...]).astype(BF16)
    tm = x.shape[0]

    def proj(lo, hi):
        return jnp.dot(h, w_ref[:, lo:hi], preferred_element_type=F32)

    def head_norm(blk, gain, scale):
        return blk * _rinv(blk, LANE) * gain * scale

    def emit_dilated(z, t, gains_ref, scale):
        for g, (_, dilation) in enumerate(DIL_GROUPS):
            out_ref = dil_refs[t * N_DIL + g]
            dst = out_ref if dilation == 1 else stage_refs[t * (N_DIL - 1) + g - 1]
            for hd in range(DIL_HEADS):
                col = (g * DIL_HEADS + hd) * LANE
                blk = z[:, col:col + LANE]
                if gains_ref is not None:
                    blk = head_norm(blk, gains_ref[:, col:col + LANE], scale)
                if dilation == 1:
                    dst[:, hd * LANE:(hd + 1) * LANE] = blk.astype(BF16)
                else:
                    dst[hd] = blk
            if dilation > 1:
                for r in range(dilation):
                    for hd in range(DIL_HEADS):
                        rows = dst[hd, pl.ds(r, tm // dilation, stride=dilation), :]
                        out_ref[r, :, hd * LANE:(hd + 1) * LANE] = rows.astype(BF16)

    rc, rs1, rs2 = rc_ref[...], rs1_ref[...], rs2_ref[...]
    cq = proj(COL_CQ, COL_CKV)
    ckv = proj(COL_CKV, COL_KR)
    k_rope = proj(COL_KR, COL_MQ)
    cq = (cq * _rinv(cq, MLA_Q_LORA) * gqa_ref[...]).astype(BF16)
    ckv = (ckv * _rinv(ckv, MLA_KV_LORA) * gkva_ref[...]).astype(BF16)
    q = jnp.dot(cq, wqb_ref[...], preferred_element_type=F32)
    kv = jnp.dot(ckv, wkvb_ref[...], preferred_element_type=F32)
    ones_lane = (lax.broadcasted_iota(jnp.int32, (1, LANE), 1) == MLA_V).astype(F32)

    def latent_heads(lo, hi):
        for hd in range(lo, hi):
            sl = slice(hd * LANE, (hd + 1) * LANE)
            qh = q[:, sl]
            qh = qh * _rinv(qh, MLA_QK) * gmq_ref[...]
            qm_ref[:, sl] = (_rope(qh, rc, rs1, rs2) * MLA_QSCALE).astype(BF16)
            kh = kv[:, sl] + k_rope
            kh = kh * _rinv(kh, MLA_QK) * gmk_ref[...]
            km_ref[:, sl] = _rope(kh, rc, rs1, rs2).astype(BF16)
            vm_ref[:, sl] = (kv[:, MLA_PAD + hd * LANE:MLA_PAD + (hd + 1) * LANE] + ones_lane).astype(BF16)

    quarter = MLA_HEADS // 4
    emit_dilated(proj(COL_DQ, COL_DK), 0, gdq_ref, DIL_QSCALE)
    latent_heads(0, quarter)
    emit_dilated(proj(COL_DK, COL_DV), 1, gdk_ref, 1.0)
    latent_heads(quarter, 2 * quarter)
    emit_dilated(proj(COL_DV, COL_CQ), 2, None, 1.0)
    latent_heads(2 * quarter, 3 * quarter)
    zmq = proj(COL_MQ, COL_GL)
    for hd in range(MEM_HEADS):
        sl = slice(hd * LANE, (hd + 1) * LANE)
        mq_ref[:, sl] = head_norm(zmq[:, sl], gmemq_ref[:, sl], MEM_QSCALE).astype(BF16)
    latent_heads(3 * quarter, MLA_HEADS)
    gl_ref[...] = proj(COL_GL, COL_END).astype(BF16)


def _in_proj(xf, p, batch, seq):
    tokens = xf.shape[0]
    tm = TOKEN_TILE
    pos_blocks = seq // tm
    row = lambda w: pl.BlockSpec((tm, w), lambda i: (i, 0))
    rope_spec = pl.BlockSpec((tm, LANE), lambda i: (i % pos_blocks, 0))
    out_specs, out_shape, scratch = [], [], []
    for _ in range(3):
        for _, dilation in DIL_GROUPS:
            if dilation == 1:
                out_specs.append(row(DIL_OUT))
                out_shape.append(jax.ShapeDtypeStruct((tokens, DIL_OUT), BF16))
            else:
                assert tm % (dilation * 16) == 0
                out_specs.append(pl.BlockSpec((None, dilation, tm // dilation, DIL_OUT),
                                              lambda i: (i // pos_blocks, 0, i % pos_blocks, 0)))
                out_shape.append(jax.ShapeDtypeStruct((batch, dilation, seq // dilation, DIL_OUT), BF16))
                scratch.append(pltpu.VMEM((DIL_HEADS, tm, LANE), F32))
    for w in (MLA_PAD, MLA_PAD, MLA_PAD, MEM_WIDTH, GATE_WIDTH):
        out_specs.append(row(w))
        out_shape.append(jax.ShapeDtypeStruct((tokens, w), BF16))
    return pl.pallas_call(
        _in_kernel,
        grid=(tokens // tm,),
        in_specs=[row(D_MODEL), _resident((1, D_MODEL)), _resident((D_MODEL, COL_END)),
                  _resident((1, DIL_WIDTH)), _resident((1, DIL_WIDTH)),
                  _resident((1, MLA_Q_LORA)), _resident((1, MLA_KV_LORA)),
                  _resident((MLA_Q_LORA, MLA_PAD)), _resident((MLA_KV_LORA, 2 * MLA_PAD)),
                  _resident((1, LANE)), _resident((1, LANE)), _resident((1, MEM_WIDTH)),
                  rope_spec, rope_spec, rope_spec],
        out_specs=out_specs,
        out_shape=out_shape,
        scratch_shapes=scratch,
        compiler_params=_params(("parallel",)),
        name="in_proj",
    )(xf, p["g_mix"], p["w_in"], p["g_dq"], p["g_dk"], p["g_qa"], p["g_kva"], p["w_qb"], p["w_kvb"],
      p["g_mq"], p["g_mk"], p["g_memq"], p["rope_c"], p["rope_s1"], p["rope_s2"])


def _dil_group(qkv_refs, state_in, o_ref, l_ref, *, slopes, dilation, sub_len, tl, sub):
    q_ref, kp_ref, kc_ref, kn_ref, vp_ref, vc_ref, vn_ref = qkv_refs
    has_prev = state_in is not None
    last = l_ref is None
    if has_prev:
        oin_ref, lin_ref = state_in

    def window(prev_ref, cur_ref, next_ref, r, r0, sl):
        lo, hi = r0 - DIL_R, r0 + sub + DIL_R
        parts = [prev_ref[r, :, sl]] if lo < 0 else []
        parts.append(cur_ref[r, max(lo, 0):min(hi, tl), sl])
        if hi > tl:
            parts.append(next_ref[r, :, sl])
        return parts[0] if len(parts) == 1 else jnp.concatenate(parts, axis=0)

    nkeys = sub + 2 * DIL_R
    qi = lax.broadcasted_iota(jnp.int32, (sub, nkeys), 0)
    kj = lax.broadcasted_iota(jnp.int32, (sub, nkeys), 1)
    absrel = jnp.abs(kj - DIL_R - qi)
    band = absrel <= DIL_R
    dist = absrel.astype(F32)
    block_start = pl.program_id(1) * tl
    valid = []
    for sb in range(tl // sub):
        upos = kj + (block_start + (sb * sub - DIL_R))
        valid.append(band & (upos >= 0) & (upos < sub_len))

    units = [(r, sb) for r in range(dilation) for sb in range(tl // sub)]
    batch = max(1, DIL_BATCH_ROWS // sub)
    for u0 in range(0, len(units), batch):
        blocks = [(r, sb, hd) for r, sb in units[u0:u0 + batch] for hd in range(DIL_HEADS)]
        cols = lambda hd: slice(hd * LANE, (hd + 1) * LANE)
        scores = []
        for r, sb, hd in blocks:
            r0 = sb * sub
            s = lax.dot_general(q_ref[r, r0:r0 + sub, cols(hd)], window(kp_ref, kc_ref, kn_ref, r, r0, cols(hd)),
                                NT_DIMS, preferred_element_type=F32)
            bias = dist * (-float(slopes[hd]) * dilation * LOG2E)
            scores.append(jnp.where(valid[sb], s + bias, NEG_INF))
        stats = []
        for s in scores:
            m = jnp.max(s, axis=-1, keepdims=True)
            pr = jnp.exp2(s - m)
            stats.append((m, pr.astype(BF16), jnp.sum(pr, axis=-1, keepdims=True)))
        outs = []
        for (r, sb, hd), (m, pr, den) in zip(blocks, stats):
            r0 = sb * sub
            o = jnp.dot(pr, window(vp_ref, vc_ref, vn_ref, r, r0, cols(hd)), preferred_element_type=F32) / den
            outs.append((o, m + jnp.log2(den)))
        for (r, sb, hd), (o, lse) in zip(blocks, outs):
            r0 = sb * sub
            if dilation == 1:
                rows = slice(r0, r0 + sub)
            else:
                rows = pl.ds(r0 * dilation + r, sub, stride=dilation)
            if has_prev:
                o_prev = oin_ref[hd, rows, :]
                l_prev = lin_ref[hd, rows, :]
                top = jnp.maximum(l_prev, lse)
                w_prev = jnp.exp2(l_prev - top)
                w_cur = jnp.exp2(lse - top)
                tot = w_prev + w_cur
                o = (w_prev * o_prev + w_cur * o) / tot
                lse = top + jnp.log2(tot)
            o_ref[hd, rows, :] = o
            if not last:
                l_ref[hd, rows, :] = jnp.broadcast_to(lse, (sub, LANE))


def _dil_geometry(g, seq):
    window, dilation = DIL_GROUPS[g]
    assert window // (2 * dilation) == DIL_R
    sub_len = seq // dilation
    assert sub_len * dilation == seq and sub_len % DIL_R == 0
    tl = min(sub_len, DIL_TOKENS // dilation)
    sub = min(tl, DIL_SUB)
    assert sub_len % tl == 0 and tl % sub == 0 and tl % DIL_R == 0
    return dict(slopes=_alibi_slopes()[g], dilation=dilation, sub_len=sub_len, tl=tl, sub=sub)


def _dil_kernel(*refs, geometry):
    n_in = 7 * N_DIL
    o_ref = refs[n_in]
    o_st, l_st = refs[n_in + 1:]
    for g, geo in enumerate(geometry):
        first, last = g == 0, g == N_DIL - 1
        _dil_group(refs[7 * g:7 * g + 7], None if first else (o_st, l_st),
                   o_ref if last else o_st, None if last else l_st, **geo)


def _dilated(dil_qkv, batch, seq):
    block = min(seq, DIL_TOKENS)
    geometry = [_dil_geometry(g, seq) for g in range(N_DIL)]
    args, in_specs = [], []
    for g, geo in enumerate(geometry):
        dilation, tl, sub_len = geo["dilation"], geo["tl"], geo["sub_len"]
        assert tl * dilation == block
        edge_blocks = sub_len // DIL_R
        per_tl = tl // DIL_R
        cur_spec = pl.BlockSpec((None, dilation, tl, DIL_OUT), lambda b, i: (b, 0, i, 0))
        prev_spec = pl.BlockSpec((None, dilation, DIL_R, DIL_OUT),
                                 lambda b, i, per_tl=per_tl: (b, 0, jnp.maximum(i * per_tl - 1, 0), 0))
        next_spec = pl.BlockSpec(
            (None, dilation, DIL_R, DIL_OUT),
            lambda b, i, per_tl=per_tl, edge_blocks=edge_blocks:
                (b, 0, jnp.minimum((i + 1) * per_tl, edge_blocks - 1), 0))
        q, k, v = (dil_qkv[t * N_DIL + g].reshape(batch, dilation, sub_len, DIL_OUT) for t in range(3))
        args += [q, k, k, k, v, v, v]
        in_specs += [cur_spec, prev_spec, cur_spec, next_spec, prev_spec, cur_spec, next_spec]
    scratch = [pltpu.VMEM((DIL_HEADS, block, LANE), F32)] * 2
    return pl.pallas_call(
        functools.partial(_dil_kernel, geometry=geometry),
        grid=(batch, seq // block),
        in_specs=in_specs,
        out_specs=pl.BlockSpec((None, DIL_HEADS, block, LANE), lambda b, i: (b, 0, i, 0)),
        out_shape=jax.ShapeDtypeStruct((batch, DIL_HEADS, seq, LANE), F32),
        scratch_shapes=scratch,
        compiler_params=_params(("parallel", "parallel")),
        name="dilated_attn",
    )(*args)


def _mla_kernel(q_ref, k_ref, v_ref, o_ref, s_buf, p_buf, *, tq, tk, nk):
    assert nk >= 2 and nk % 2 == 0
    for grp in range(MLA_HEADS // MLA_GROUP):
        cols = [slice(hd * LANE, (hd + 1) * LANE) for hd in range(grp * MLA_GROUP, (grp + 1) * MLA_GROUP)]
        qs = [q_ref[:, sl] for sl in cols]

        def tile(j):
            return pl.ds(pl.multiple_of(j * tk, tk), tk)

        def scores(j, h, slot):
            s = lax.dot_general(k_ref[tile(j), cols[h]], qs[h], NT_DIMS, preferred_element_type=F32)
            s_buf[slot, h] = s
            return jnp.max(s, axis=0, keepdims=True)

        def numerators(h, slot, m, tile_max):
            m_new = jnp.maximum(m, tile_max)
            p_buf[slot, h] = jnp.exp2(s_buf[slot, h] - m_new).astype(BF16)
            return m_new, jnp.exp2(m - m_new)

        def accumulate(j, h, slot, alpha, acc):
            pv = lax.dot_general(v_ref[tile(j), cols[h]], p_buf[slot, h], TN_DIMS,
                                 preferred_element_type=F32)
            return alpha * acc + pv

        def trip(j, slot, state):
            out = []
            for h, (m, alpha, tile_max, acc) in enumerate(state):
                acc = accumulate(j - 1, h, 1 - slot, alpha, acc)
                m, alpha = numerators(h, slot, m, tile_max)
                out.append((m, alpha, scores(j + 1, h, 1 - slot), acc))
            return tuple(out)

        state = []
        for h in range(MLA_GROUP):
            m0 = jnp.full((1, tq), -jnp.inf, F32)
            m, alpha = numerators(h, 0, m0, scores(0, h, 0))
            state.append((m, alpha, scores(1, h, 1), jnp.zeros((LANE, tq), F32)))

        def body(jj, state):
            j = 2 * jj + 1
            return trip(j + 1, 0, trip(j, 1, state))

        state = lax.fori_loop(0, (nk - 2) // 2, body, tuple(state))
        last = (nk - 1) % 2
        tops = []
        for h, (m, alpha, tile_max, acc) in enumerate(state):
            acc = accumulate(nk - 2, h, 1 - last, alpha, acc)
            m, alpha = numerators(h, last, m, tile_max)
            acc = accumulate(nk - 1, h, last, alpha, acc)
            tops.append((acc / acc[MLA_V:MLA_V + 1, :])[:MLA_V])
        for i in range(MLA_GROUP // 2):
            col = (grp * MLA_GROUP // 2 + i) * LANE
            o_ref[:, col:col + LANE] = jnp.concatenate(tops[2 * i:2 * i + 2], axis=0).T.astype(BF16)


def _mla(qm, km, vm, batch, seq):
    tq, tk = MLA_TQ, MLA_TK
    view = lambda t: t.reshape(batch, seq, MLA_PAD)
    kv_spec = pl.BlockSpec((None, seq, MLA_PAD), lambda b, i: (b, 0, 0))
    out = pl.pallas_call(
        functools.partial(_mla_kernel, tq=tq, tk=tk, nk=seq // tk),
        grid=(batch, seq // tq),
        in_specs=[pl.BlockSpec((None, tq, MLA_PAD), lambda b, i: (b, i, 0)), kv_spec, kv_spec],
        out_specs=pl.BlockSpec((None, tq, MLA_HEADS * MLA_V), lambda b, i: (b, i, 0)),
        out_shape=jax.ShapeDtypeStruct((batch, seq, MLA_HEADS * MLA_V), BF16),
        scratch_shapes=[pltpu.VMEM((2, MLA_GROUP, tk, tq), F32), pltpu.VMEM((2, MLA_GROUP, tk, tq), BF16)],
        compiler_params=_params(("parallel", "arbitrary")),
        name="latent_attn",
    )(view(qm), view(km), view(vm))
    return out.reshape(batch * seq, MLA_HEADS * MLA_V)


def _memkv_kernel(mem_ref, g_ref, w_ref, gk_ref, k_ref, v_ref):
    x = mem_ref[...]
    h = (x * _rinv(x, D_MODEL) * g_ref[...]).astype(BF16)
    kv = jnp.dot(h, w_ref[...], preferred_element_type=F32)
    for hd in range(MEM_HEADS):
        sl = slice(hd * LANE, (hd + 1) * LANE)
        blk = kv[:, sl]
        k_ref[:, sl] = (blk * _rinv(blk, MEM_HEAD_DIM) * gk_ref[...]).astype(BF16)
    v_ref[...] = kv[:, MEM_WIDTH:].astype(BF16)


def _mem_kv(mem, p):
    batch, n_mem, _ = mem.shape
    spec = lambda w: pl.BlockSpec((None, n_mem, w), lambda b: (b, 0, 0))
    return pl.pallas_call(
        _memkv_kernel,
        grid=(batch,),
        in_specs=[spec(D_MODEL), _resident((1, D_MODEL)), _resident((D_MODEL, 2 * MEM_WIDTH)),
                  _resident((1, LANE))],
        out_specs=[spec(MEM_WIDTH), spec(MEM_WIDTH)],
        out_shape=[jax.ShapeDtypeStruct((batch, n_mem, MEM_WIDTH), BF16)] * 2,
        compiler_params=_params(("parallel",)),
        name="memory_kv",
    )(mem, p["g_mem"], p["w_memkv"], p["g_memk"])


def _memattn_kernel(q_ref, k_ref, v_ref, o_ref):
    for hd in range(MEM_HEADS):
        sl = slice(hd * LANE, (hd + 1) * LANE)
        s = lax.dot_general(q_ref[:, sl], k_ref[:, sl], NT_DIMS, preferred_element_type=F32)
        m = jnp.max(s, axis=-1, keepdims=True)
        pr = jnp.exp2(s - m)
        den = jnp.sum(pr, axis=-1, keepdims=True)
        o = jnp.dot(pr.astype(BF16), v_ref[:, sl], preferred_element_type=F32) / den
        o_ref[:, sl] = o.astype(BF16)


def _mem_attn(mq, kmem, vmem, batch, seq):
    n_mem = kmem.shape[1]
    tq = MEM_TQ
    kv_spec = pl.BlockSpec((None, n_mem, MEM_WIDTH), lambda b, i: (b, 0, 0))
    q_spec = pl.BlockSpec((None, tq, MEM_WIDTH), lambda b, i: (b, i, 0))
    out = pl.pallas_call(
        _memattn_kernel,
        grid=(batch, seq // tq),
        in_specs=[q_spec, kv_spec, kv_spec],
        out_specs=q_spec,
        out_shape=jax.ShapeDtypeStruct((batch, seq, MEM_WIDTH), BF16),
        compiler_params=_params(("parallel", "parallel")),
        name="memory_attn",
    )(mq.reshape(batch, seq, MEM_WIDTH), kmem, vmem)
    return out.reshape(batch * seq, MEM_WIDTH)


def _post_kernel(x_ref, a_ref, b_ref, m_ref, gl_ref, wb_ref, wo_ref, gffn_ref, w1_ref, w2_ref, o_ref):
    a = jnp.concatenate([a_ref[hd] for hd in range(DIL_HEADS)], axis=-1).astype(BF16)
    mixed = None
    for i, br in enumerate((a, b_ref[...], m_ref[...])):
        proj = jnp.dot(br, wb_ref[i], preferred_element_type=F32)
        gate = jax.nn.sigmoid(gl_ref[:, i * D_MODEL:(i + 1) * D_MODEL].astype(F32))
        mixed = gate * proj if mixed is None else mixed + gate * proj
    x1 = x_ref[...] + jnp.dot(mixed.astype(BF16), wo_ref[...], preferred_element_type=F32)
    h2 = (x1 * _rinv(x1, D_MODEL) * gffn_ref[...]).astype(BF16)
    acc = x1
    for c in range(D_FF // FF_CHUNK):
        sl = slice(c * FF_CHUNK, (c + 1) * FF_CHUNK)
        u = jnp.dot(h2, w1_ref[:, sl], preferred_element_type=F32)
        u = jnp.square(jnp.maximum(u, 0.0)).astype(BF16)
        acc = acc + jnp.dot(u, w2_ref[sl, :], preferred_element_type=F32)
    o_ref[...] = acc


def _post(xf, a, b, m, gl, p):
    tokens = xf.shape[0]
    tm = TOKEN_TILE
    row = lambda w: pl.BlockSpec((tm, w), lambda i: (i, 0))
    seq_blocks = a.shape[2] // tm
    a_spec = pl.BlockSpec((None, DIL_HEADS, tm, LANE), lambda i: (i // seq_blocks, 0, i % seq_blocks, 0))
    return pl.pallas_call(
        _post_kernel,
        grid=(tokens // tm,),
        in_specs=[row(D_MODEL), a_spec, row(BRANCH_WIDTH), row(BRANCH_WIDTH), row(GATE_WIDTH),
                  _resident((N_BRANCH, BRANCH_WIDTH, D_MODEL)), _resident((D_MODEL, D_MODEL)),
                  _resident((1, D_MODEL)), _resident((D_MODEL, D_FF)), _resident((D_FF, D_MODEL))],
        out_specs=row(D_MODEL),
        out_shape=jax.ShapeDtypeStruct((tokens, D_MODEL), F32),
        compiler_params=_params(("parallel",)),
        name="merge_ffn",
    )(xf, a, b, m, gl, p["w_branch"], p["w_out"], p["g_ffn"], p["w_ff1"], p["w_ff2"])


def _rope_tables(seq):
    half = MLA_ROPE // 2
    inv = ROPE_THETA ** (-jnp.arange(half, dtype=F32) * 2.0 / MLA_ROPE)
    ang = jnp.arange(seq, dtype=F32)[:, None] * inv[None, :]
    c, s = jnp.cos(ang), jnp.sin(ang)
    z = lambda w: jnp.zeros((seq, w), F32)
    tail = LANE - MLA_QK
    rope_c = jnp.concatenate([jnp.ones((seq, MLA_NOPE), F32), c, c, z(tail)], axis=-1)
    rope_s1 = jnp.concatenate([z(MLA_NOPE), -s, z(half), z(tail)], axis=-1)
    rope_s2 = jnp.concatenate([z(MLA_NOPE), z(half), s, z(tail)], axis=-1)
    return rope_c, rope_s1, rope_s2


def _pad_lanes(t, width):
    return jnp.pad(t, [(0, 0)] * (t.ndim - 1) + [(0, width - t.shape[-1])])


def _prep_layer(l, mix_norm, w_in, dil_q_norm, dil_k_norm, mla_q_a_norm, mla_kv_a_norm, w_mla_q_b,
                w_mla_kv_b, mla_q_norm, mla_k_norm, mem_norm, w_mem_kv, mem_q_norm, mem_k_norm,
                w_branch, w_out, ffn_norm, w_ff1, w_ff2):
    w = w_in[l]
    o_cq = 3 * DIL_WIDTH
    o_ckv = o_cq + MLA_Q_LORA
    o_kr = o_ckv + MLA_KV_LORA
    o_mq = o_kr + MLA_ROPE
    o_gl = o_mq + MEM_WIDTH
    kr = jnp.pad(w[:, o_kr:o_mq], ((0, 0), (MLA_NOPE, LANE - MLA_QK)))
    w_r = jnp.concatenate([w[:, :o_kr], kr, w[:, o_mq:]], axis=1).astype(BF16)
    w_qb = _pad_lanes(w_mla_q_b[l].reshape(MLA_Q_LORA, MLA_HEADS, MLA_QK), LANE)
    w_kvb = w_mla_kv_b[l].reshape(MLA_KV_LORA, MLA_HEADS, MLA_NOPE + MLA_V)
    w_kb = _pad_lanes(w_kvb[:, :, :MLA_NOPE], LANE).reshape(MLA_KV_LORA, MLA_PAD)
    w_vb = _pad_lanes(w_kvb[:, :, MLA_NOPE:], LANE).reshape(MLA_KV_LORA, MLA_PAD)
    return {
        "g_mix": mix_norm[l][None],
        "w_in": w_r,
        "g_dq": dil_q_norm[l].reshape(1, DIL_WIDTH),
        "g_dk": dil_k_norm[l].reshape(1, DIL_WIDTH),
        "g_qa": mla_q_a_norm[l][None],
        "g_kva": mla_kv_a_norm[l][None],
        "w_qb": w_qb.reshape(MLA_Q_LORA, MLA_PAD).astype(BF16),
        "w_kvb": jnp.concatenate([w_kb, w_vb], axis=1).astype(BF16),
        "g_mq": _pad_lanes(mla_q_norm[l][None], LANE),
        "g_mk": _pad_lanes(mla_k_norm[l][None], LANE),
        "g_memq": jnp.tile(mem_q_norm[l][None], (1, MEM_HEADS)),
        "g_mem": mem_norm[l][None],
        "w_memkv": w_mem_kv[l].astype(BF16),
        "g_memk": mem_k_norm[l][None],
        "w_branch": w_branch[l].astype(BF16),
        "w_out": w_out[l].astype(BF16),
        "g_ffn": ffn_norm[l][None],
        "w_ff1": w_ff1[l].astype(BF16),
        "w_ff2": w_ff2[l].astype(BF16),
    }


def _encoder_layer(x, mem, p):
    batch, seq, _ = x.shape
    xf = x.reshape(batch * seq, D_MODEL)
    outs = _in_proj(xf, p, batch, seq)
    dil_qkv, (qm, km, vm, mq, gl) = outs[:3 * N_DIL], outs[3 * N_DIL:]
    a = _dilated(dil_qkv, batch, seq)
    b = _mla(qm, km, vm, batch, seq)
    kmem, vmem = _mem_kv(mem, p)
    m = _mem_attn(mq, kmem, vmem, batch, seq)
    return _post(xf, a, b, m, gl, p).reshape(batch, seq, D_MODEL)


def kernel(x_prompt, x_sample, mem_prompt, mem_sample, mix_norm, w_in, dil_q_norm, dil_k_norm, mla_q_a_norm, mla_kv_a_norm, w_mla_q_b, w_mla_kv_b, mla_q_norm, mla_k_norm, mem_norm, w_mem_kv, mem_q_norm, mem_k_norm, w_branch, w_out, ffn_norm, w_ff1, w_ff2):
    weights = (mix_norm, w_in, dil_q_norm, dil_k_norm, mla_q_a_norm, mla_kv_a_norm, w_mla_q_b,
               w_mla_kv_b, mla_q_norm, mla_k_norm, mem_norm, w_mem_kv, mem_q_norm, mem_k_norm,
               w_branch, w_out, ffn_norm, w_ff1, w_ff2)
    yp, ys = x_prompt, x_sample
    rope_p = _rope_tables(yp.shape[1])
    rope_s = _rope_tables(ys.shape[1])
    for l in range(mix_norm.shape[0]):
        p = _prep_layer(l, *weights)
        names = ("rope_c", "rope_s1", "rope_s2")
        yp = _encoder_layer(yp, mem_prompt, {**p, **dict(zip(names, rope_p))})
        ys = _encoder_layer(ys, mem_sample, {**p, **dict(zip(names, rope_s))})
    return (yp, ys)
```

```python
import functools
import math

import numpy as np
import jax
import jax.numpy as jnp
from jax import lax
from jax.experimental import pallas as pl
from jax.experimental.pallas import tpu as pltpu

F32 = jnp.float32
BF16 = jnp.bfloat16

LANE = 128
VMEM_LIMIT_BYTES = 56 * 1024 * 1024

D_MODEL = 1024
EPS = 1e-6
NEG_INF = -1e30
LOG2E = math.log2(math.e)
DIL_GROUPS = ((128, 1), (512, 4), (2048, 16))
N_DIL = 3
DIL_HEADS = 4
DIL_HEAD_DIM = 128
DIL_WIDTH = N_DIL * DIL_HEADS * DIL_HEAD_DIM
DIL_OUT = DIL_HEADS * DIL_HEAD_DIM
MLA_HEADS = 8
MLA_Q_LORA = 384
MLA_KV_LORA = 256
MLA_NOPE = 64
MLA_ROPE = 32
MLA_QK = MLA_NOPE + MLA_ROPE
MLA_V = 64
MLA_PAD = MLA_HEADS * LANE
ROPE_THETA = 10000.0
MEM_HEADS = 4
MEM_HEAD_DIM = 128
MEM_WIDTH = MEM_HEADS * MEM_HEAD_DIM
N_BRANCH = 3
BRANCH_WIDTH = 512
D_FF = 4 * D_MODEL
GATE_WIDTH = N_BRANCH * D_MODEL

COL_DQ = 0
COL_DK = COL_DQ + DIL_WIDTH
COL_DV = COL_DK + DIL_WIDTH
COL_CQ = COL_DV + DIL_WIDTH
COL_CKV = COL_CQ + MLA_Q_LORA
COL_KR = COL_CKV + MLA_KV_LORA
COL_MQ = COL_KR + LANE
COL_END = COL_MQ + MEM_WIDTH

DIL_R = 64
DIL_SUB = 128
DIL_TOKENS = 1024
DIL_BATCH_ROWS = 256
DIL_QSCALE = LOG2E / math.sqrt(DIL_HEAD_DIM)
MLA_QSCALE = LOG2E / math.sqrt(MLA_QK)
MEM_QSCALE = LOG2E / math.sqrt(MEM_HEAD_DIM)

IN_TILE = 256
POST_TILE = 512
MLA_TQ = 256
MLA_TK = 512
MLA_GROUP = 8
MEM_TQ = 512
FF_CHUNK = 1024

NT_DIMS = (((1,), (1,)), ((), ()))
TN_DIMS = (((0,), (0,)), ((), ()))


def _alibi_slopes():
    n = N_DIL * DIL_HEADS
    s = np.array([2.0 ** (-8.0 * (k + 1) / n) for k in range(n)], dtype=np.float32)
    return s.reshape(N_DIL, DIL_HEADS)


def _rinv(x, width):
    return lax.rsqrt(jnp.sum(x * x, axis=-1, keepdims=True) * (1.0 / width) + EPS)


def _resident(shape):
    nd = len(shape)
    return pl.BlockSpec(shape, lambda *_: (0,) * nd, pipeline_mode=pl.Buffered(1))


def _params(semantics):
    return pltpu.CompilerParams(dimension_semantics=semantics, vmem_limit_bytes=VMEM_LIMIT_BYTES)


def _rope(t, c, s1, s2):
    return t * c + pltpu.roll(t, LANE - MLA_ROPE // 2, 1) * s1 + pltpu.roll(t, MLA_ROPE // 2, 1) * s2


def _in_kernel(x_ref, gmix_ref, w_ref, gdq_ref, gdk_ref, gqa_ref, gkva_ref, wqb_ref, wkvb_ref,
               gmq_ref, gmk_ref, gmemq_ref, rc_ref, rs1_ref, rs2_ref, *refs):
    dil_refs, (qm_ref, km_ref, vm_ref, mq_ref) = refs[:3 * N_DIL], refs[3 * N_DIL:3 * N_DIL + 4]
    stage_refs = refs[3 * N_DIL + 4:]
    x = x_ref[...]
    h = (x * _rinv(x, D_MODEL) * gmix_ref[...]).astype(BF16)
    tm = x.shape[0]

    def proj(lo, hi):
        return jnp.dot(h, w_ref[:, lo:hi], preferred_element_type=F32)

    def head_norm(blk, gain, scale):
        return blk * _rinv(blk, LANE) * gain * scale

    def emit_dilated(z, t, gains_ref, scale):
        for g, (_, dilation) in enumerate(DIL_GROUPS):
            out_ref = dil_refs[t * N_DIL + g]
            dst = out_ref if dilation == 1 else stage_refs[t * (N_DIL - 1) + g - 1]
            for hd in range(DIL_HEADS):
                col = (g * DIL_HEADS + hd) * LANE
                blk = z[:, col:col + LANE]
                if gains_ref is not None:
                    blk = head_norm(blk, gains_ref[:, col:col + LANE], scale)
                if dilation == 1:
                    dst[:, hd * LANE:(hd + 1) * LANE] = blk.astype(BF16)
                else:
                    dst[hd] = blk
            if dilation > 1:
                for r in range(dilation):
                    for hd in range(DIL_HEADS):
                        rows = dst[hd, pl.ds(r, tm // dilation, stride=dilation), :]
                        out_ref[r, :, hd * LANE:(hd + 1) * LANE] = rows.astype(BF16)

    rc, rs1, rs2 = rc_ref[...], rs1_ref[...], rs2_ref[...]
    cq = proj(COL_CQ, COL_CKV)
    ckv = proj(COL_CKV, COL_KR)
    k_rope = proj(COL_KR, COL_MQ)
    cq = (cq * _rinv(cq, MLA_Q_LORA) * gqa_ref[...]).astype(BF16)
    ckv = (ckv * _rinv(ckv, MLA_KV_LORA) * gkva_ref[...]).astype(BF16)
    q = jnp.dot(cq, wqb_ref[...], preferred_element_type=F32)
    kv = jnp.dot(ckv, wkvb_ref[...], preferred_element_type=F32)
    ones_lane = (lax.broadcasted_iota(jnp.int32, (1, LANE), 1) == MLA_V).astype(F32)

    def latent_heads(lo, hi):
        for hd in range(lo, hi):
            sl = slice(hd * LANE, (hd + 1) * LANE)
            qh = q[:, sl]
            qh = qh * _rinv(qh, MLA_QK) * gmq_ref[...]
            qm_ref[:, sl] = (_rope(qh, rc, rs1, rs2) * MLA_QSCALE).astype(BF16)
            kh = kv[:, sl] + k_rope
            kh = kh * _rinv(kh, MLA_QK) * gmk_ref[...]
            km_ref[:, sl] = _rope(kh, rc, rs1, rs2).astype(BF16)
            vm_ref[:, sl] = (kv[:, MLA_PAD + hd * LANE:MLA_PAD + (hd + 1) * LANE] + ones_lane).astype(BF16)

    quarter = MLA_HEADS // 4
    emit_dilated(proj(COL_DQ, COL_DK), 0, gdq_ref, DIL_QSCALE)
    latent_heads(0, quarter)
    emit_dilated(proj(COL_DK, COL_DV), 1, gdk_ref, 1.0)
    latent_heads(quarter, 2 * quarter)
    emit_dilated(proj(COL_DV, COL_CQ), 2, None, 1.0)
    latent_heads(2 * quarter, 3 * quarter)
    zmq = proj(COL_MQ, COL_END)
    for hd in range(MEM_HEADS):
        sl = slice(hd * LANE, (hd + 1) * LANE)
        mq_ref[:, sl] = head_norm(zmq[:, sl], gmemq_ref[:, sl], MEM_QSCALE).astype(BF16)
    latent_heads(3 * quarter, MLA_HEADS)


def _in_proj(xf, p, batch, seq):
    tokens = xf.shape[0]
    tm = IN_TILE
    pos_blocks = seq // tm
    row = lambda w: pl.BlockSpec((tm, w), lambda i: (i, 0))
    rope_spec = pl.BlockSpec((tm, LANE), lambda i: (i % pos_blocks, 0))
    out_specs, out_shape, scratch = [], [], []
    for _ in range(3):
        for _, dilation in DIL_GROUPS:
            if dilation == 1:
                out_specs.append(row(DIL_OUT))
                out_shape.append(jax.ShapeDtypeStruct((tokens, DIL_OUT), BF16))
            else:
                assert tm % (dilation * 16) == 0
                out_specs.append(pl.BlockSpec((None, dilation, tm // dilation, DIL_OUT),
                                              lambda i: (i // pos_blocks, 0, i % pos_blocks, 0)))
                out_shape.append(jax.ShapeDtypeStruct((batch, dilation, seq // dilation, DIL_OUT), BF16))
                scratch.append(pltpu.VMEM((DIL_HEADS, tm, LANE), F32))
    for w in (MLA_PAD, MLA_PAD, MLA_PAD, MEM_WIDTH):
        out_specs.append(row(w))
        out_shape.append(jax.ShapeDtypeStruct((tokens, w), BF16))
    return pl.pallas_call(
        _in_kernel,
        grid=(tokens // tm,),
        in_specs=[row(D_MODEL), _resident((1, D_MODEL)), _resident((D_MODEL, COL_END)),
                  _resident((1, DIL_WIDTH)), _resident((1, DIL_WIDTH)),
                  _resident((1, MLA_Q_LORA)), _resident((1, MLA_KV_LORA)),
                  _resident((MLA_Q_LORA, MLA_PAD)), _resident((MLA_KV_LORA, 2 * MLA_PAD)),
                  _resident((1, LANE)), _resident((1, LANE)), _resident((1, MEM_WIDTH)),
                  rope_spec, rope_spec, rope_spec],
        out_specs=out_specs,
        out_shape=out_shape,
        scratch_shapes=scratch,
        compiler_params=_params(("parallel",)),
        name="in_proj",
    )(xf, p["g_mix"], p["w_in"], p["g_dq"], p["g_dk"], p["g_qa"], p["g_kva"], p["w_qb"], p["w_kvb"],
      p["g_mq"], p["g_mk"], p["g_memq"], p["rope_c"], p["rope_s1"], p["rope_s2"])


def _dil_group(qkv_refs, state_in, o_ref, l_ref, *, slopes, dilation, sub_len, tl, sub):
    q_ref, kp_ref, kc_ref, kn_ref, vp_ref, vc_ref, vn_ref = qkv_refs
    has_prev = state_in is not None
    last = l_ref is None
    if has_prev:
        oin_ref, lin_ref = state_in

    def window(prev_ref, cur_ref, next_ref, r, r0, sl):
        lo, hi = r0 - DIL_R, r0 + sub + DIL_R
        parts = [prev_ref[r, :, sl]] if lo < 0 else []
        parts.append(cur_ref[r, max(lo, 0):min(hi, tl), sl])
        if hi > tl:
            parts.append(next_ref[r, :, sl])
        return parts[0] if len(parts) == 1 else jnp.concatenate(parts, axis=0)

    nkeys = sub + 2 * DIL_R
    qi = lax.broadcasted_iota(jnp.int32, (sub, nkeys), 0)
    kj = lax.broadcasted_iota(jnp.int32, (sub, nkeys), 1)
    absrel = jnp.abs(kj - DIL_R - qi)
    band = absrel <= DIL_R
    dist = absrel.astype(F32)
    block_start = pl.program_id(1) * tl
    valid = []
    for sb in range(tl // sub):
        upos = kj + (block_start + (sb * sub - DIL_R))
        valid.append(band & (upos >= 0) & (upos < sub_len))

    units = [(r, sb) for r in range(dilation) for sb in range(tl // sub)]
    batch = max(1, DIL_BATCH_ROWS // sub)
    for u0 in range(0, len(units), batch):
        blocks = [(r, sb, hd) for r, sb in units[u0:u0 + batch] for hd in range(DIL_HEADS)]
        cols = lambda hd: slice(hd * LANE, (hd + 1) * LANE)
        scores = []
        for r, sb, hd in blocks:
            r0 = sb * sub
            s = lax.dot_general(q_ref[r, r0:r0 + sub, cols(hd)], window(kp_ref, kc_ref, kn_ref, r, r0, cols(hd)),
                                NT_DIMS, preferred_element_type=F32)
            bias = dist * (-float(slopes[hd]) * dilation * LOG2E)
            scores.append(jnp.where(valid[sb], s + bias, NEG_INF))
        stats = []
        for s in scores:
            m = jnp.max(s, axis=-1, keepdims=True)
            pr = jnp.exp2(s - m)
            stats.append((m, pr.astype(BF16), jnp.sum(pr, axis=-1, keepdims=True)))
        outs = []
        for (r, sb, hd), (m, pr, den) in zip(blocks, stats):
            r0 = sb * sub
            o = jnp.dot(pr, window(vp_ref, vc_ref, vn_ref, r, r0, cols(hd)), preferred_element_type=F32) / den
            outs.append((o, m + jnp.log2(den)))
        for (r, sb, hd), (o, lse) in zip(blocks, outs):
            r0 = sb * sub
            if dilation == 1:
                rows = slice(r0, r0 + sub)
            else:
                rows = pl.ds(r0 * dilation + r, sub, stride=dilation)
            if has_prev:
                o_prev = oin_ref[hd, rows, :]
                l_prev = lin_ref[hd, rows, :]
                top = jnp.maximum(l_prev, lse)
                w_prev = jnp.exp2(l_prev - top)
                w_cur = jnp.exp2(lse - top)
                tot = w_prev + w_cur
                o = (w_prev * o_prev + w_cur * o) / tot
                lse = top + jnp.log2(tot)
            o_ref[hd, rows, :] = o
            if not last:
                l_ref[hd, rows, :] = jnp.broadcast_to(lse, (sub, LANE))


def _dil_geometry(g, seq):
    window, dilation = DIL_GROUPS[g]
    assert window // (2 * dilation) == DIL_R
    sub_len = seq // dilation
    assert sub_len * dilation == seq and sub_len % DIL_R == 0
    tl = min(sub_len, DIL_TOKENS // dilation)
    sub = min(tl, DIL_SUB)
    assert sub_len % tl == 0 and tl % sub == 0 and tl % DIL_R == 0
    return dict(slopes=_alibi_slopes()[g], dilation=dilation, sub_len=sub_len, tl=tl, sub=sub)


def _dil_kernel(*refs, geometry):
    n_in = 7 * N_DIL
    o_ref = refs[n_in]
    o_st, l_st = refs[n_in + 1:]
    for g, geo in enumerate(geometry):
        first, last = g == 0, g == N_DIL - 1
        _dil_group(refs[7 * g:7 * g + 7], None if first else (o_st, l_st),
                   o_ref if last else o_st, None if last else l_st, **geo)


def _dilated(dil_qkv, batch, seq):
    block = min(seq, DIL_TOKENS)
    geometry = [_dil_geometry(g, seq) for g in range(N_DIL)]
    args, in_specs = [], []
    for g, geo in enumerate(geometry):
        dilation, tl, sub_len = geo["dilation"], geo["tl"], geo["sub_len"]
        assert tl * dilation == block
        edge_blocks = sub_len // DIL_R
        per_tl = tl // DIL_R
        cur_spec = pl.BlockSpec((None, dilation, tl, DIL_OUT), lambda b, i: (b, 0, i, 0))
        prev_spec = pl.BlockSpec((None, dilation, DIL_R, DIL_OUT),
                                 lambda b, i, per_tl=per_tl: (b, 0, jnp.maximum(i * per_tl - 1, 0), 0))
        next_spec = pl.BlockSpec(
            (None, dilation, DIL_R, DIL_OUT),
            lambda b, i, per_tl=per_tl, edge_blocks=edge_blocks:
                (b, 0, jnp.minimum((i + 1) * per_tl, edge_blocks - 1), 0))
        q, k, v = (dil_qkv[t * N_DIL + g].reshape(batch, dilation, sub_len, DIL_OUT) for t in range(3))
        args += [q, k, k, k, v, v, v]
        in_specs += [cur_spec, prev_spec, cur_spec, next_spec, prev_spec, cur_spec, next_spec]
    scratch = [pltpu.VMEM((DIL_HEADS, block, LANE), F32)] * 2
    return pl.pallas_call(
        functools.partial(_dil_kernel, geometry=geometry),
        grid=(batch, seq // block),
        in_specs=in_specs,
        out_specs=pl.BlockSpec((None, DIL_HEADS, block, LANE), lambda b, i: (b, 0, i, 0)),
        out_shape=jax.ShapeDtypeStruct((batch, DIL_HEADS, seq, LANE), F32),
        scratch_shapes=scratch,
        compiler_params=_params(("parallel", "parallel")),
        name="dilated_attn",
    )(*args)


def _mla_kernel(q_ref, k_ref, v_ref, o_ref, s_buf, p_buf, *, tq, tk, nk):
    assert nk >= 2 and nk % 2 == 0
    for grp in range(MLA_HEADS // MLA_GROUP):
        cols = [slice(hd * LANE, (hd + 1) * LANE) for hd in range(grp * MLA_GROUP, (grp + 1) * MLA_GROUP)]
        qs = [q_ref[:, sl] for sl in cols]

        def tile(j):
            return slice(j * tk, (j + 1) * tk)

        def scores(j, h, slot):
            s = lax.dot_general(k_ref[tile(j), cols[h]], qs[h], NT_DIMS, preferred_element_type=F32)
            s_buf[slot, h] = s
            return jnp.max(s, axis=0, keepdims=True)

        def numerators(h, slot, m, tile_max):
            m_new = jnp.maximum(m, tile_max)
            p_buf[slot, h] = jnp.exp2(s_buf[slot, h] - m_new).astype(BF16)
            return m_new, jnp.exp2(m - m_new)

        def accumulate(j, h, slot, alpha, acc):
            pv = lax.dot_general(v_ref[tile(j), cols[h]], p_buf[slot, h], TN_DIMS,
                                 preferred_element_type=F32)
            return alpha * acc + pv

        def trip(j, slot, state):
            out = []
            for h, (m, alpha, tile_max, acc) in enumerate(state):
                acc = accumulate(j - 1, h, 1 - slot, alpha, acc)
                m, alpha = numerators(h, slot, m, tile_max)
                out.append((m, alpha, scores(j + 1, h, 1 - slot), acc))
            return tuple(out)

        state = []
        for h in range(MLA_GROUP):
            m0 = jnp.full((1, tq), -jnp.inf, F32)
            m, alpha = numerators(h, 0, m0, scores(0, h, 0))
            state.append((m, alpha, scores(1, h, 1), jnp.zeros((LANE, tq), F32)))

        for j in range(1, nk - 1):
            state = trip(j, j % 2, state)
        last = (nk - 1) % 2
        tops = []
        for h, (m, alpha, tile_max, acc) in enumerate(state):
            acc = accumulate(nk - 2, h, 1 - last, alpha, acc)
            m, alpha = numerators(h, last, m, tile_max)
            acc = accumulate(nk - 1, h, last, alpha, acc)
            tops.append((acc / acc[MLA_V:MLA_V + 1, :])[:MLA_V])
        for i in range(MLA_GROUP // 2):
            col = (grp * MLA_GROUP // 2 + i) * LANE
            o_ref[:, col:col + LANE] = jnp.concatenate(tops[2 * i:2 * i + 2], axis=0).T.astype(BF16)


def _mla(qm, km, vm, batch, seq):
    tq, tk = MLA_TQ, MLA_TK
    view = lambda t: t.reshape(batch, seq, MLA_PAD)
    kv_spec = pl.BlockSpec((None, seq, MLA_PAD), lambda b, i: (b, 0, 0))
    out = pl.pallas_call(
        functools.partial(_mla_kernel, tq=tq, tk=tk, nk=seq // tk),
        grid=(batch, seq // tq),
        in_specs=[pl.BlockSpec((None, tq, MLA_PAD), lambda b, i: (b, i, 0)), kv_spec, kv_spec],
        out_specs=pl.BlockSpec((None, tq, MLA_HEADS * MLA_V), lambda b, i: (b, i, 0)),
        out_shape=jax.ShapeDtypeStruct((batch, seq, MLA_HEADS * MLA_V), BF16),
        scratch_shapes=[pltpu.VMEM((2, MLA_GROUP, tk, tq), F32), pltpu.VMEM((2, MLA_GROUP, tk, tq), BF16)],
        compiler_params=_params(("parallel", "arbitrary")),
        name="latent_attn",
    )(view(qm), view(km), view(vm))
    return out.reshape(batch * seq, MLA_HEADS * MLA_V)


def _memkv_kernel(mem_ref, g_ref, w_ref, gk_ref, k_ref, v_ref):
    x = mem_ref[...]
    h = (x * _rinv(x, D_MODEL) * g_ref[...]).astype(BF16)
    kv = jnp.dot(h, w_ref[...], preferred_element_type=F32)
    for hd in range(MEM_HEADS):
        sl = slice(hd * LANE, (hd + 1) * LANE)
        blk = kv[:, sl]
        k_ref[:, sl] = (blk * _rinv(blk, MEM_HEAD_DIM) * gk_ref[...]).astype(BF16)
    v_ref[...] = kv[:, MEM_WIDTH:].astype(BF16)


def _mem_kv(mem, p):
    batch, n_mem, _ = mem.shape
    spec = lambda w: pl.BlockSpec((None, n_mem, w), lambda b: (b, 0, 0))
    return pl.pallas_call(
        _memkv_kernel,
        grid=(batch,),
        in_specs=[spec(D_MODEL), _resident((1, D_MODEL)), _resident((D_MODEL, 2 * MEM_WIDTH)),
                  _resident((1, LANE))],
        out_specs=[spec(MEM_WIDTH), spec(MEM_WIDTH)],
        out_shape=[jax.ShapeDtypeStruct((batch, n_mem, MEM_WIDTH), BF16)] * 2,
        compiler_params=_params(("parallel",)),
        name="memory_kv",
    )(mem, p["g_mem"], p["w_memkv"], p["g_memk"])


def _memattn_kernel(q_ref, k_ref, v_ref, o_ref):
    for hd in range(MEM_HEADS):
        sl = slice(hd * LANE, (hd + 1) * LANE)
        s = lax.dot_general(q_ref[:, sl], k_ref[:, sl], NT_DIMS, preferred_element_type=F32)
        m = jnp.max(s, axis=-1, keepdims=True)
        pr = jnp.exp2(s - m)
        den = jnp.sum(pr, axis=-1, keepdims=True)
        o = jnp.dot(pr.astype(BF16), v_ref[:, sl], preferred_element_type=F32) / den
        o_ref[:, sl] = o.astype(BF16)


def _mem_attn(mq, kmem, vmem, batch, seq):
    n_mem = kmem.shape[1]
    tq = MEM_TQ
    kv_spec = pl.BlockSpec((None, n_mem, MEM_WIDTH), lambda b, i: (b, 0, 0))
    q_spec = pl.BlockSpec((None, tq, MEM_WIDTH), lambda b, i: (b, i, 0))
    out = pl.pallas_call(
        _memattn_kernel,
        grid=(batch, seq // tq),
        in_specs=[q_spec, kv_spec, kv_spec],
        out_specs=q_spec,
        out_shape=jax.ShapeDtypeStruct((batch, seq, MEM_WIDTH), BF16),
        compiler_params=_params(("parallel", "parallel")),
        name="memory_attn",
    )(mq.reshape(batch, seq, MEM_WIDTH), kmem, vmem)
    return out.reshape(batch * seq, MEM_WIDTH)


def _post_kernel(x_ref, a_ref, b_ref, m_ref, gmix_ref, wg_ref, wb_ref, wo_ref, gffn_ref, w1_ref, w2_ref,
                 o_ref):
    x = x_ref[...]
    h = (x * _rinv(x, D_MODEL) * gmix_ref[...]).astype(BF16)
    a = jnp.concatenate([a_ref[hd] for hd in range(DIL_HEADS)], axis=-1).astype(BF16)
    mixed = None
    for i, br in enumerate((a, b_ref[...], m_ref[...])):
        proj = jnp.dot(br, wb_ref[i], preferred_element_type=F32)
        logits = jnp.dot(h, wg_ref[:, i * D_MODEL:(i + 1) * D_MODEL], preferred_element_type=F32)
        gate = jax.nn.sigmoid(logits)
        mixed = gate * proj if mixed is None else mixed + gate * proj
    x1 = x + jnp.dot(mixed.astype(BF16), wo_ref[...], preferred_element_type=F32)
    h2 = (x1 * _rinv(x1, D_MODEL) * gffn_ref[...]).astype(BF16)
    acc = x1
    for c in range(D_FF // FF_CHUNK):
        sl = slice(c * FF_CHUNK, (c + 1) * FF_CHUNK)
        u = jnp.dot(h2, w1_ref[:, sl], preferred_element_type=F32)
        u = jnp.square(jnp.maximum(u, 0.0)).astype(BF16)
        acc = acc + jnp.dot(u, w2_ref[sl, :], preferred_element_type=F32)
    o_ref[...] = acc


def _post(xf, a, b, m, p):
    tokens = xf.shape[0]
    tm = POST_TILE
    row = lambda w: pl.BlockSpec((tm, w), lambda i: (i, 0))
    seq_blocks = a.shape[2] // tm
    a_spec = pl.BlockSpec((None, DIL_HEADS, tm, LANE), lambda i: (i // seq_blocks, 0, i % seq_blocks, 0))
    return pl.pallas_call(
        _post_kernel,
        grid=(tokens // tm,),
        in_specs=[row(D_MODEL), a_spec, row(BRANCH_WIDTH), row(BRANCH_WIDTH),
                  _resident((1, D_MODEL)), _resident((D_MODEL, GATE_WIDTH)),
                  _resident((N_BRANCH, BRANCH_WIDTH, D_MODEL)), _resident((D_MODEL, D_MODEL)),
                  _resident((1, D_MODEL)), _resident((D_MODEL, D_FF)), _resident((D_FF, D_MODEL))],
        out_specs=row(D_MODEL),
        out_shape=jax.ShapeDtypeStruct((tokens, D_MODEL), F32),
        compiler_params=_params(("parallel",)),
        name="merge_ffn",
    )(xf, a, b, m, p["g_mix"], p["w_gate"], p["w_branch"], p["w_out"], p["g_ffn"], p["w_ff1"], p["w_ff2"])


def _rope_tables(seq):
    half = MLA_ROPE // 2
    inv = ROPE_THETA ** (-jnp.arange(half, dtype=F32) * 2.0 / MLA_ROPE)
    ang = jnp.arange(seq, dtype=F32)[:, None] * inv[None, :]
    c, s = jnp.cos(ang), jnp.sin(ang)
    z = lambda w: jnp.zeros((seq, w), F32)
    tail = LANE - MLA_QK
    rope_c = jnp.concatenate([jnp.ones((seq, MLA_NOPE), F32), c, c, z(tail)], axis=-1)
    rope_s1 = jnp.concatenate([z(MLA_NOPE), -s, z(half), z(tail)], axis=-1)
    rope_s2 = jnp.concatenate([z(MLA_NOPE), z(half), s, z(tail)], axis=-1)
    return rope_c, rope_s1, rope_s2


def _pad_lanes(t, width):
    return jnp.pad(t, [(0, 0)] * (t.ndim - 1) + [(0, width - t.shape[-1])])


def _prep_layer(l, mix_norm, w_in, dil_q_norm, dil_k_norm, mla_q_a_norm, mla_kv_a_norm, w_mla_q_b,
                w_mla_kv_b, mla_q_norm, mla_k_norm, mem_norm, w_mem_kv, mem_q_norm, mem_k_norm,
                w_branch, w_out, ffn_norm, w_ff1, w_ff2):
    w = w_in[l]
    o_cq = 3 * DIL_WIDTH
    o_ckv = o_cq + MLA_Q_LORA
    o_kr = o_ckv + MLA_KV_LORA
    o_mq = o_kr + MLA_ROPE
    o_gl = o_mq + MEM_WIDTH
    kr = jnp.pad(w[:, o_kr:o_mq], ((0, 0), (MLA_NOPE, LANE - MLA_QK)))
    w_r = jnp.concatenate([w[:, :o_kr], kr, w[:, o_mq:o_gl]], axis=1).astype(BF16)
    w_qb = _pad_lanes(w_mla_q_b[l].reshape(MLA_Q_LORA, MLA_HEADS, MLA_QK), LANE)
    w_kvb = w_mla_kv_b[l].reshape(MLA_KV_LORA, MLA_HEADS, MLA_NOPE + MLA_V)
    w_kb = _pad_lanes(w_kvb[:, :, :MLA_NOPE], LANE).reshape(MLA_KV_LORA, MLA_PAD)
    w_vb = _pad_lanes(w_kvb[:, :, MLA_NOPE:], LANE).reshape(MLA_KV_LORA, MLA_PAD)
    return {
        "g_mix": mix_norm[l][None],
        "w_in": w_r,
        "w_gate": w[:, o_gl:].astype(BF16),
        "g_dq": dil_q_norm[l].reshape(1, DIL_WIDTH),
        "g_dk": dil_k_norm[l].reshape(1, DIL_WIDTH),
        "g_qa": mla_q_a_norm[l][None],
        "g_kva": mla_kv_a_norm[l][None],
        "w_qb": w_qb.reshape(MLA_Q_LORA, MLA_PAD).astype(BF16),
        "w_kvb": jnp.concatenate([w_kb, w_vb], axis=1).astype(BF16),
        "g_mq": _pad_lanes(mla_q_norm[l][None], LANE),
        "g_mk": _pad_lanes(mla_k_norm[l][None], LANE),
        "g_memq": jnp.tile(mem_q_norm[l][None], (1, MEM_HEADS)),
        "g_mem": mem_norm[l][None],
        "w_memkv": w_mem_kv[l].astype(BF16),
        "g_memk": mem_k_norm[l][None],
        "w_branch": w_branch[l].astype(BF16),
        "w_out": w_out[l].astype(BF16),
        "g_ffn": ffn_norm[l][None],
        "w_ff1": w_ff1[l].astype(BF16),
        "w_ff2": w_ff2[l].astype(BF16),
    }


def _encoder_layer(x, mem, p):
    batch, seq, _ = x.shape
    xf = x.reshape(batch * seq, D_MODEL)
    outs = _in_proj(xf, p, batch, seq)
    dil_qkv, (qm, km, vm, mq) = outs[:3 * N_DIL], outs[3 * N_DIL:]
    a = _dilated(dil_qkv, batch, seq)
    b = _mla(qm, km, vm, batch, seq)
    kmem, vmem = _mem_kv(mem, p)
    m = _mem_attn(mq, kmem, vmem, batch, seq)
    return _post(xf, a, b, m, p).reshape(batch, seq, D_MODEL)


def kernel(x_prompt, x_sample, mem_prompt, mem_sample, mix_norm, w_in, dil_q_norm, dil_k_norm, mla_q_a_norm, mla_kv_a_norm, w_mla_q_b, w_mla_kv_b, mla_q_norm, mla_k_norm, mem_norm, w_mem_kv, mem_q_norm, mem_k_norm, w_branch, w_out, ffn_norm, w_ff1, w_ff2):
    weights = (mix_norm, w_in, dil_q_norm, dil_k_norm, mla_q_a_norm, mla_kv_a_norm, w_mla_q_b,
               w_mla_kv_b, mla_q_norm, mla_k_norm, mem_norm, w_mem_kv, mem_q_norm, mem_k_norm,
               w_branch, w_out, ffn_norm, w_ff1, w_ff2)
    yp, ys = x_prompt, x_sample
    rope_p = _rope_tables(yp.shape[1])
    rope_s = _rope_tables(ys.shape[1])
    for l in range(mix_norm.shape[0]):
        p = _prep_layer(l, *weights)
        names = ("rope_c", "rope_s1", "rope_s2")
        yp = _encoder_layer(yp, mem_prompt, {**p, **dict(zip(names, rope_p))})
        ys = _encoder_layer(ys, mem_sample, {**p, **dict(zip(names, rope_s))})
    return (yp, ys)
```

```python
import functools
import math

import numpy as np
import jax
import jax.numpy as jnp
from jax import lax
from jax.experimental import pallas as pl
from jax.experimental.pallas import tpu as pltpu

F32 = jnp.float32
BF16 = jnp.bfloat16

LANE = 128
VMEM_LIMIT_BYTES = 56 * 1024 * 1024

D_MODEL = 1024
EPS = 1e-6
NEG_INF = -1e30
LOG2E = math.log2(math.e)
DIL_GROUPS = ((128, 1), (512, 4), (2048, 16))
N_DIL = 3
DIL_HEADS = 4
DIL_HEAD_DIM = 128
DIL_WIDTH = N_DIL * DIL_HEADS * DIL_HEAD_DIM
DIL_OUT = DIL_HEADS * DIL_HEAD_DIM
MLA_HEADS = 8
MLA_Q_LORA = 384
MLA_KV_LORA = 256
MLA_NOPE = 64
MLA_ROPE = 32
MLA_QK = MLA_NOPE + MLA_ROPE
MLA_V = 64
MLA_PAD = MLA_HEADS * LANE
ROPE_THETA = 10000.0
MEM_HEADS = 4
MEM_HEAD_DIM = 128
MEM_WIDTH = MEM_HEADS * MEM_HEAD_DIM
N_BRANCH = 3
BRANCH_WIDTH = 512
D_FF = 4 * D_MODEL
GATE_WIDTH = N_BRANCH * D_MODEL

COL_DQ = 0
COL_DK = COL_DQ + DIL_WIDTH
COL_DV = COL_DK + DIL_WIDTH
COL_CQ = COL_DV + DIL_WIDTH
COL_CKV = COL_CQ + MLA_Q_LORA
COL_KR = COL_CKV + MLA_KV_LORA
COL_MQ = COL_KR + LANE
COL_END = COL_MQ + MEM_WIDTH

DIL_R = 64
DIL_SUB = 128
DIL_TOKENS = 1024
DIL_BATCH_ROWS = 256
DIL_QSCALE = LOG2E / math.sqrt(DIL_HEAD_DIM)
MLA_QSCALE = LOG2E / math.sqrt(MLA_QK)
MEM_QSCALE = LOG2E / math.sqrt(MEM_HEAD_DIM)

IN_TILE = 256
POST_TILE = 512
MLA_TQ = 256
MLA_TK = 512
MLA_GROUP = 8
MLA_VROWS = 80
MEM_TQ = 512
FF_CHUNK = 1024

NT_DIMS = (((1,), (1,)), ((), ()))
TN_DIMS = (((0,), (0,)), ((), ()))


def _alibi_slopes():
    n = N_DIL * DIL_HEADS
    s = np.array([2.0 ** (-8.0 * (k + 1) / n) for k in range(n)], dtype=np.float32)
    return s.reshape(N_DIL, DIL_HEADS)


def _rinv(x, width):
    return lax.rsqrt(jnp.sum(x * x, axis=-1, keepdims=True) * (1.0 / width) + EPS)


def _resident(shape):
    nd = len(shape)
    return pl.BlockSpec(shape, lambda *_: (0,) * nd, pipeline_mode=pl.Buffered(1))


def _params(semantics):
    return pltpu.CompilerParams(dimension_semantics=semantics, vmem_limit_bytes=VMEM_LIMIT_BYTES)


def _rope(t, c, s1, s2):
    return t * c + pltpu.roll(t, LANE - MLA_ROPE // 2, 1) * s1 + pltpu.roll(t, MLA_ROPE // 2, 1) * s2


def _in_kernel(x_ref, gmix_ref, w_ref, gdq_ref, gdk_ref, gqa_ref, gkva_ref, wqb_ref, wkvb_ref,
               gmq_ref, gmk_ref, gmemq_ref, rc_ref, rs1_ref, rs2_ref, *refs):
    dil_refs, (qm_ref, km_ref, vm_ref, mq_ref) = refs[:3 * N_DIL], refs[3 * N_DIL:3 * N_DIL + 4]
    stage_refs = refs[3 * N_DIL + 4:]
    x = x_ref[...]
    h = (x * _rinv(x, D_MODEL) * gmix_ref[...]).astype(BF16)
    tm = x.shape[0]

    def proj(lo, hi):
        return jnp.dot(h, w_ref[:, lo:hi], preferred_element_type=F32)

    def head_norm(blk, gain, scale):
        return blk * _rinv(blk, LANE) * gain * scale

    def emit_dilated(z, t, gains_ref, scale):
        for g, (_, dilation) in enumerate(DIL_GROUPS):
            out_ref = dil_refs[t * N_DIL + g]
            dst = out_ref if dilation == 1 else stage_refs[t * (N_DIL - 1) + g - 1]
            for hd in range(DIL_HEADS):
                col = (g * DIL_HEADS + hd) * LANE
                blk = z[:, col:col + LANE]
                if gains_ref is not None:
                    blk = head_norm(blk, gains_ref[:, col:col + LANE], scale)
                if dilation == 1:
                    dst[:, hd * LANE:(hd + 1) * LANE] = blk.astype(BF16)
                else:
                    dst[hd] = blk
            if dilation > 1:
                for r in range(dilation):
                    for hd in range(DIL_HEADS):
                        rows = dst[hd, pl.ds(r, tm // dilation, stride=dilation), :]
                        out_ref[r, :, hd * LANE:(hd + 1) * LANE] = rows.astype(BF16)

    rc, rs1, rs2 = rc_ref[...], rs1_ref[...], rs2_ref[...]
    cq = proj(COL_CQ, COL_CKV)
    ckv = proj(COL_CKV, COL_KR)
    k_rope = proj(COL_KR, COL_MQ)
    cq = (cq * _rinv(cq, MLA_Q_LORA) * gqa_ref[...]).astype(BF16)
    ckv = (ckv * _rinv(ckv, MLA_KV_LORA) * gkva_ref[...]).astype(BF16)
    q = jnp.dot(cq, wqb_ref[...], preferred_element_type=F32)
    kv = jnp.dot(ckv, wkvb_ref[...], preferred_element_type=F32)
    ones_lane = (lax.broadcasted_iota(jnp.int32, (1, LANE), 1) == MLA_V).astype(F32)

    def latent_heads(lo, hi):
        for hd in range(lo, hi):
            sl = slice(hd * LANE, (hd + 1) * LANE)
            qh = q[:, sl]
            qh = qh * _rinv(qh, MLA_QK) * gmq_ref[...]
            qm_ref[:, sl] = (_rope(qh, rc, rs1, rs2) * MLA_QSCALE).astype(BF16)
            kh = kv[:, sl] + k_rope
            kh = kh * _rinv(kh, MLA_QK) * gmk_ref[...]
            km_ref[:, sl] = _rope(kh, rc, rs1, rs2).astype(BF16)
            vm_ref[:, sl] = (kv[:, MLA_PAD + hd * LANE:MLA_PAD + (hd + 1) * LANE] + ones_lane).astype(BF16)

    quarter = MLA_HEADS // 4
    emit_dilated(proj(COL_DQ, COL_DK), 0, gdq_ref, DIL_QSCALE)
    latent_heads(0, quarter)
    emit_dilated(proj(COL_DK, COL_DV), 1, gdk_ref, 1.0)
    latent_heads(quarter, 2 * quarter)
    emit_dilated(proj(COL_DV, COL_CQ), 2, None, 1.0)
    latent_heads(2 * quarter, 3 * quarter)
    zmq = proj(COL_MQ, COL_END)
    for hd in range(MEM_HEADS):
        sl = slice(hd * LANE, (hd + 1) * LANE)
        mq_ref[:, sl] = head_norm(zmq[:, sl], gmemq_ref[:, sl], MEM_QSCALE).astype(BF16)
    latent_heads(3 * quarter, MLA_HEADS)


def _in_proj(xf, p, batch, seq):
    tokens = xf.shape[0]
    tm = IN_TILE
    pos_blocks = seq // tm
    row = lambda w: pl.BlockSpec((tm, w), lambda i: (i, 0))
    rope_spec = pl.BlockSpec((tm, LANE), lambda i: (i % pos_blocks, 0))
    out_specs, out_shape, scratch = [], [], []
    for _ in range(3):
        for _, dilation in DIL_GROUPS:
            if dilation == 1:
                out_specs.append(row(DIL_OUT))
                out_shape.append(jax.ShapeDtypeStruct((tokens, DIL_OUT), BF16))
            else:
                assert tm % (dilation * 16) == 0
                out_specs.append(pl.BlockSpec((None, dilation, tm // dilation, DIL_OUT),
                                              lambda i: (i // pos_blocks, 0, i % pos_blocks, 0)))
                out_shape.append(jax.ShapeDtypeStruct((batch, dilation, seq // dilation, DIL_OUT), BF16))
                scratch.append(pltpu.VMEM((DIL_HEADS, tm, LANE), F32))
    for w in (MLA_PAD, MLA_PAD, MLA_PAD, MEM_WIDTH):
        out_specs.append(row(w))
        out_shape.append(jax.ShapeDtypeStruct((tokens, w), BF16))
    return pl.pallas_call(
        _in_kernel,
        grid=(tokens // tm,),
        in_specs=[row(D_MODEL), _resident((1, D_MODEL)), _resident((D_MODEL, COL_END)),
                  _resident((1, DIL_WIDTH)), _resident((1, DIL_WIDTH)),
                  _resident((1, MLA_Q_LORA)), _resident((1, MLA_KV_LORA)),
                  _resident((MLA_Q_LORA, MLA_PAD)), _resident((MLA_KV_LORA, 2 * MLA_PAD)),
                  _resident((1, LANE)), _resident((1, LANE)), _resident((1, MEM_WIDTH)),
                  rope_spec, rope_spec, rope_spec],
        out_specs=out_specs,
        out_shape=out_shape,
        scratch_shapes=scratch,
        compiler_params=_params(("parallel",)),
        name="in_proj",
    )(xf, p["g_mix"], p["w_in"], p["g_dq"], p["g_dk"], p["g_qa"], p["g_kva"], p["w_qb"], p["w_kvb"],
      p["g_mq"], p["g_mk"], p["g_memq"], p["rope_c"], p["rope_s1"], p["rope_s2"])


def _dil_group(qkv_refs, state, o_ref, *, first, slopes, dilation, sub_len, tl, sub):
    q_ref, kp_ref, kc_ref, kn_ref, vp_ref, vc_ref, vn_ref = qkv_refs
    acc_st, m_st, d_st = state
    last = o_ref is not None

    def window(prev_ref, cur_ref, next_ref, r, r0, sl):
        lo, hi = r0 - DIL_R, r0 + sub + DIL_R
        parts = [prev_ref[r, :, sl]] if lo < 0 else []
        parts.append(cur_ref[r, max(lo, 0):min(hi, tl), sl])
        if hi > tl:
            parts.append(next_ref[r, :, sl])
        return parts[0] if len(parts) == 1 else jnp.concatenate(parts, axis=0)

    nkeys = sub + 2 * DIL_R
    qi = lax.broadcasted_iota(jnp.int32, (sub, nkeys), 0)
    kj = lax.broadcasted_iota(jnp.int32, (sub, nkeys), 1)
    absrel = jnp.abs(kj - DIL_R - qi)
    band = absrel <= DIL_R
    dist = absrel.astype(F32)
    block_start = pl.program_id(1) * tl
    n_sub = tl // sub
    bias_cache = {}

    def masked_bias(sb, hd):
        edge = sb == 0 or sb == n_sub - 1
        key = (sb if edge else -1, hd)
        if key not in bias_cache:
            ok = band
            if edge:
                upos = kj + (block_start + (sb * sub - DIL_R))
                ok = band & (upos >= 0) & (upos < sub_len)
            bias_cache[key] = jnp.where(ok, dist * (-float(slopes[hd]) * dilation * LOG2E), NEG_INF)
        return bias_cache[key]

    units = [(r, sb) for r in range(dilation) for sb in range(tl // sub)]
    batch = max(1, DIL_BATCH_ROWS // sub)
    cols = lambda hd: slice(hd * LANE, (hd + 1) * LANE)

    def score_stage(blocks):
        out = []
        for r, sb, hd in blocks:
            r0 = sb * sub
            s = lax.dot_general(q_ref[r, r0:r0 + sub, cols(hd)], window(kp_ref, kc_ref, kn_ref, r, r0, cols(hd)),
                                NT_DIMS, preferred_element_type=F32)
            out.append(s + masked_bias(sb, hd))
        return out

    def softmax_stage(blocks, scores):
        out = []
        for s in scores:
            m = jnp.max(s, axis=-1, keepdims=True)
            pr = jnp.exp2(s - m)
            out.append((m, pr.astype(BF16)))
        return out

    def value_stage(blocks, stats):
        out = []
        ones = jnp.ones((nkeys, LANE), BF16)
        for (r, sb, hd), (m, pr) in zip(blocks, stats):
            r0 = sb * sub
            v_ext = jnp.concatenate([window(vp_ref, vc_ref, vn_ref, r, r0, cols(hd)), ones], axis=-1)
            res = jnp.dot(pr, v_ext, preferred_element_type=F32)
            out.append((res[:, :LANE], m, res[:, LANE:]))
        return out

    def merge_stage(blocks, outs):
        for (r, sb, hd), (acc, m, den) in zip(blocks, outs):
            r0 = sb * sub
            if dilation == 1:
                rows = slice(r0, r0 + sub)
            else:
                rows = pl.ds(r0 * dilation + r, sub, stride=dilation)
            if not first:
                m_prev = m_st[hd, rows, :]
                m_new = jnp.maximum(m_prev, m)
                w_prev = jnp.exp2(m_prev - m_new)
                w_cur = jnp.exp2(m - m_new)
                acc = w_prev * acc_st[hd, rows, :] + w_cur * acc
                den = w_prev * d_st[hd, rows, :] + w_cur * den
                m = m_new
            if last:
                o_ref[hd, rows, :] = acc / den
            else:
                acc_st[hd, rows, :] = acc
                m_st[hd, rows, :] = jnp.broadcast_to(m, (sub, LANE))
                d_st[hd, rows, :] = jnp.broadcast_to(den, (sub, LANE))

    stages = (score_stage, softmax_stage, value_stage, merge_stage)
    return [(stages, [(r, sb, hd) for r, sb in units[u0:u0 + batch] for hd in range(DIL_HEADS)])
            for u0 in range(0, len(units), batch)]


def _dil_geometry(g, seq):
    window, dilation = DIL_GROUPS[g]
    assert window // (2 * dilation) == DIL_R
    sub_len = seq // dilation
    assert sub_len * dilation == seq and sub_len % DIL_R == 0
    tl = min(sub_len, DIL_TOKENS // dilation)
    sub = min(tl, DIL_SUB)
    assert sub_len % tl == 0 and tl % sub == 0 and tl % DIL_R == 0
    return dict(slopes=_alibi_slopes()[g], dilation=dilation, sub_len=sub_len, tl=tl, sub=sub)


def _dil_kernel(*refs, geometry):
    n_in = 7 * N_DIL
    o_ref = refs[n_in]
    state = refs[n_in + 1:]
    work = []
    for g, geo in enumerate(geometry):
        work += _dil_group(refs[7 * g:7 * g + 7], state, o_ref if g == N_DIL - 1 else None, first=g == 0, **geo)
    n_stage = 4
    carried = [None] * len(work)
    for t in range(len(work) + n_stage - 1):
        for s in reversed(range(n_stage)):
            i = t - s
            if 0 <= i < len(work):
                stages, blocks = work[i]
                carried[i] = stages[s](blocks) if s == 0 else stages[s](blocks, carried[i])


def _dilated(dil_qkv, batch, seq):
    block = min(seq, DIL_TOKENS)
    geometry = [_dil_geometry(g, seq) for g in range(N_DIL)]
    args, in_specs = [], []
    for g, geo in enumerate(geometry):
        dilation, tl, sub_len = geo["dilation"], geo["tl"], geo["sub_len"]
        assert tl * dilation == block
        edge_blocks = sub_len // DIL_R
        per_tl = tl // DIL_R
        cur_spec = pl.BlockSpec((None, dilation, tl, DIL_OUT), lambda b, i: (b, 0, i, 0))
        prev_spec = pl.BlockSpec((None, dilation, DIL_R, DIL_OUT),
                                 lambda b, i, per_tl=per_tl: (b, 0, jnp.maximum(i * per_tl - 1, 0), 0))
        next_spec = pl.BlockSpec(
            (None, dilation, DIL_R, DIL_OUT),
            lambda b, i, per_tl=per_tl, edge_blocks=edge_blocks:
                (b, 0, jnp.minimum((i + 1) * per_tl, edge_blocks - 1), 0))
        q, k, v = (dil_qkv[t * N_DIL + g].reshape(batch, dilation, sub_len, DIL_OUT) for t in range(3))
        args += [q, k, k, k, v, v, v]
        in_specs += [cur_spec, prev_spec, cur_spec, next_spec, prev_spec, cur_spec, next_spec]
    scratch = [pltpu.VMEM((DIL_HEADS, block, LANE), F32)] * 3
    return pl.pallas_call(
        functools.partial(_dil_kernel, geometry=geometry),
        grid=(batch, seq // block),
        in_specs=in_specs,
        out_specs=pl.BlockSpec((None, DIL_HEADS, block, LANE), lambda b, i: (b, 0, i, 0)),
        out_shape=jax.ShapeDtypeStruct((batch, DIL_HEADS, seq, LANE), F32),
        scratch_shapes=scratch,
        compiler_params=_params(("parallel", "parallel")),
        name="dilated_attn",
    )(*args)


def _mla_kernel(q_ref, k_ref, v_ref, o_ref, s_buf, p_buf, *, tq, tk, nk):
    assert nk >= 2 and nk % 2 == 0
    for grp in range(MLA_HEADS // MLA_GROUP):
        cols = [slice(hd * LANE, (hd + 1) * LANE) for hd in range(grp * MLA_GROUP, (grp + 1) * MLA_GROUP)]
        qs = [q_ref[:, sl] for sl in cols]

        def tile(j):
            return slice(j * tk, (j + 1) * tk)

        def scores(j, h, slot):
            s = lax.dot_general(k_ref[tile(j), cols[h]], qs[h], NT_DIMS, preferred_element_type=F32)
            s_buf[slot, h] = s
            return jnp.max(s, axis=0, keepdims=True)

        def numerators(h, slot, m, tile_max):
            m_new = jnp.maximum(m, tile_max)
            p_buf[slot, h] = jnp.exp2(s_buf[slot, h] - m_new).astype(BF16)
            return m_new, jnp.exp2(m - m_new)

        def accumulate(j, h, slot, alpha, acc):
            vt = v_ref[tile(j), cols[h]].T[:MLA_VROWS]
            pv = jnp.dot(vt, p_buf[slot, h], preferred_element_type=F32)
            return alpha * acc + pv

        def trip(j, slot, state):
            out = []
            for h, (m, alpha, tile_max, acc) in enumerate(state):
                acc = accumulate(j - 1, h, 1 - slot, alpha, acc)
                m, alpha = numerators(h, slot, m, tile_max)
                out.append((m, alpha, scores(j + 1, h, 1 - slot), acc))
            return tuple(out)

        state = []
        for h in range(MLA_GROUP):
            m0 = jnp.full((1, tq), -jnp.inf, F32)
            m, alpha = numerators(h, 0, m0, scores(0, h, 0))
            state.append((m, alpha, scores(1, h, 1), jnp.zeros((MLA_VROWS, tq), F32)))

        for j in range(1, nk - 1):
            state = trip(j, j % 2, state)
        last = (nk - 1) % 2
        tops = []
        for h, (m, alpha, tile_max, acc) in enumerate(state):
            acc = accumulate(nk - 2, h, 1 - last, alpha, acc)
            m, alpha = numerators(h, last, m, tile_max)
            acc = accumulate(nk - 1, h, last, alpha, acc)
            tops.append((acc / acc[MLA_V:MLA_V + 1, :])[:MLA_V])
        for i in range(MLA_GROUP // 2):
            col = (grp * MLA_GROUP // 2 + i) * LANE
            o_ref[:, col:col + LANE] = jnp.concatenate(tops[2 * i:2 * i + 2], axis=0).T.astype(BF16)


def _mla(qm, km, vm, batch, seq):
    tq, tk = MLA_TQ, MLA_TK
    view = lambda t: t.reshape(batch, seq, MLA_PAD)
    kv_spec = pl.BlockSpec((None, seq, MLA_PAD), lambda b, i: (b, 0, 0))
    out = pl.pallas_call(
        functools.partial(_mla_kernel, tq=tq, tk=tk, nk=seq // tk),
        grid=(batch, seq // tq),
        in_specs=[pl.BlockSpec((None, tq, MLA_PAD), lambda b, i: (b, i, 0)), kv_spec, kv_spec],
        out_specs=pl.BlockSpec((None, tq, MLA_HEADS * MLA_V), lambda b, i: (b, i, 0)),
        out_shape=jax.ShapeDtypeStruct((batch, seq, MLA_HEADS * MLA_V), BF16),
        scratch_shapes=[pltpu.VMEM((2, MLA_GROUP, tk, tq), F32), pltpu.VMEM((2, MLA_GROUP, tk, tq), BF16)],
        compiler_params=_params(("parallel", "arbitrary")),
        name="latent_attn",
    )(view(qm), view(km), view(vm))
    return out.reshape(batch * seq, MLA_HEADS * MLA_V)


def _memkv_kernel(mem_ref, g_ref, w_ref, gk_ref, k_ref, v_ref):
    x = mem_ref[...]
    h = (x * _rinv(x, D_MODEL) * g_ref[...]).astype(BF16)
    kv = jnp.dot(h, w_ref[...], preferred_element_type=F32)
    for hd in range(MEM_HEADS):
        sl = slice(hd * LANE, (hd + 1) * LANE)
        blk = kv[:, sl]
        k_ref[:, sl] = (blk * _rinv(blk, MEM_HEAD_DIM) * gk_ref[...]).astype(BF16)
    v_ref[...] = kv[:, MEM_WIDTH:].astype(BF16)


def _mem_kv(mem, p):
    batch, n_mem, _ = mem.shape
    spec = lambda w: pl.BlockSpec((None, n_mem, w), lambda b: (b, 0, 0))
    return pl.pallas_call(
        _memkv_kernel,
        grid=(batch,),
        in_specs=[spec(D_MODEL), _resident((1, D_MODEL)), _resident((D_MODEL, 2 * MEM_WIDTH)),
                  _resident((1, LANE))],
        out_specs=[spec(MEM_WIDTH), spec(MEM_WIDTH)],
        out_shape=[jax.ShapeDtypeStruct((batch, n_mem, MEM_WIDTH), BF16)] * 2,
        compiler_params=_params(("parallel",)),
        name="memory_kv",
    )(mem, p["g_mem"], p["w_memkv"], p["g_memk"])


def _memattn_kernel(q_ref, k_ref, v_ref, o_ref):
    for hd in range(MEM_HEADS):
        sl = slice(hd * LANE, (hd + 1) * LANE)
        s = lax.dot_general(q_ref[:, sl], k_ref[:, sl], NT_DIMS, preferred_element_type=F32)
        m = jnp.max(s, axis=-1, keepdims=True)
        pr = jnp.exp2(s - m)
        den = jnp.sum(pr, axis=-1, keepdims=True)
        o = jnp.dot(pr.astype(BF16), v_ref[:, sl], preferred_element_type=F32) / den
        o_ref[:, sl] = o.astype(BF16)


def _mem_attn(mq, kmem, vmem, batch, seq):
    n_mem = kmem.shape[1]
    tq = MEM_TQ
    kv_spec = pl.BlockSpec((None, n_mem, MEM_WIDTH), lambda b, i: (b, 0, 0))
    q_spec = pl.BlockSpec((None, tq, MEM_WIDTH), lambda b, i: (b, i, 0))
    out = pl.pallas_call(
        _memattn_kernel,
        grid=(batch, seq // tq),
        in_specs=[q_spec, kv_spec, kv_spec],
        out_specs=q_spec,
        out_shape=jax.ShapeDtypeStruct((batch, seq, MEM_WIDTH), BF16),
        compiler_params=_params(("parallel", "parallel")),
        name="memory_attn",
    )(mq.reshape(batch, seq, MEM_WIDTH), kmem, vmem)
    return out.reshape(batch * seq, MEM_WIDTH)


def _post_kernel(x_ref, a_ref, b_ref, m_ref, gmix_ref, wg_ref, wb_ref, wo_ref, gffn_ref, w1_ref, w2_ref,
                 o_ref):
    x = x_ref[...]
    h = (x * _rinv(x, D_MODEL) * gmix_ref[...]).astype(BF16)
    a = jnp.concatenate([a_ref[hd] for hd in range(DIL_HEADS)], axis=-1).astype(BF16)
    mixed = None
    for i, br in enumerate((a, b_ref[...], m_ref[...])):
        proj = jnp.dot(br, wb_ref[i], preferred_element_type=F32)
        logits = jnp.dot(h, wg_ref[:, i * D_MODEL:(i + 1) * D_MODEL], preferred_element_type=F32)
        gate = jax.nn.sigmoid(logits)
        mixed = gate * proj if mixed is None else mixed + gate * proj
    x1 = x + jnp.dot(mixed.astype(BF16), wo_ref[...], preferred_element_type=F32)
    h2 = (x1 * _rinv(x1, D_MODEL) * gffn_ref[...]).astype(BF16)
    acc = x1
    for c in range(D_FF // FF_CHUNK):
        sl = slice(c * FF_CHUNK, (c + 1) * FF_CHUNK)
        u = jnp.dot(h2, w1_ref[:, sl], preferred_element_type=F32)
        u = jnp.square(jnp.maximum(u, 0.0)).astype(BF16)
        acc = acc + jnp.dot(u, w2_ref[sl, :], preferred_element_type=F32)
    o_ref[...] = acc


def _post(xf, a, b, m, p):
    tokens = xf.shape[0]
    tm = POST_TILE
    row = lambda w: pl.BlockSpec((tm, w), lambda i: (i, 0))
    seq_blocks = a.shape[2] // tm
    a_spec = pl.BlockSpec((None, DIL_HEADS, tm, LANE), lambda i: (i // seq_blocks, 0, i % seq_blocks, 0))
    return pl.pallas_call(
        _post_kernel,
        grid=(tokens // tm,),
        in_specs=[row(D_MODEL), a_spec, row(BRANCH_WIDTH), row(BRANCH_WIDTH),
                  _resident((1, D_MODEL)), _resident((D_MODEL, GATE_WIDTH)),
                  _resident((N_BRANCH, BRANCH_WIDTH, D_MODEL)), _resident((D_MODEL, D_MODEL)),
                  _resident((1, D_MODEL)), _resident((D_MODEL, D_FF)), _resident((D_FF, D_MODEL))],
        out_specs=row(D_MODEL),
        out_shape=jax.ShapeDtypeStruct((tokens, D_MODEL), F32),
        compiler_params=_params(("parallel",)),
        name="merge_ffn",
    )(xf, a, b, m, p["g_mix"], p["w_gate"], p["w_branch"], p["w_out"], p["g_ffn"], p["w_ff1"], p["w_ff2"])


def _rope_tables(seq):
    half = MLA_ROPE // 2
    inv = ROPE_THETA ** (-jnp.arange(half, dtype=F32) * 2.0 / MLA_ROPE)
    ang = jnp.arange(seq, dtype=F32)[:, None] * inv[None, :]
    c, s = jnp.cos(ang), jnp.sin(ang)
    z = lambda w: jnp.zeros((seq, w), F32)
    tail = LANE - MLA_QK
    rope_c = jnp.concatenate([jnp.ones((seq, MLA_NOPE), F32), c, c, z(tail)], axis=-1)
    rope_s1 = jnp.concatenate([z(MLA_NOPE), -s, z(half), z(tail)], axis=-1)
    rope_s2 = jnp.concatenate([z(MLA_NOPE), z(half), s, z(tail)], axis=-1)
    return rope_c, rope_s1, rope_s2


def _pad_lanes(t, width):
    return jnp.pad(t, [(0, 0)] * (t.ndim - 1) + [(0, width - t.shape[-1])])


def _prep_layer(l, mix_norm, w_in, dil_q_norm, dil_k_norm, mla_q_a_norm, mla_kv_a_norm, w_mla_q_b,
                w_mla_kv_b, mla_q_norm, mla_k_norm, mem_norm, w_mem_kv, mem_q_norm, mem_k_norm,
                w_branch, w_out, ffn_norm, w_ff1, w_ff2):
    w = w_in[l]
    o_cq = 3 * DIL_WIDTH
    o_ckv = o_cq + MLA_Q_LORA
    o_kr = o_ckv + MLA_KV_LORA
    o_mq = o_kr + MLA_ROPE
    o_gl = o_mq + MEM_WIDTH
    kr = jnp.pad(w[:, o_kr:o_mq], ((0, 0), (MLA_NOPE, LANE - MLA_QK)))
    w_r = jnp.concatenate([w[:, :o_kr], kr, w[:, o_mq:o_gl]], axis=1).astype(BF16)
    w_qb = _pad_lanes(w_mla_q_b[l].reshape(MLA_Q_LORA, MLA_HEADS, MLA_QK), LANE)
    w_kvb = w_mla_kv_b[l].reshape(MLA_KV_LORA, MLA_HEADS, MLA_NOPE + MLA_V)
    w_kb = _pad_lanes(w_kvb[:, :, :MLA_NOPE], LANE).reshape(MLA_KV_LORA, MLA_PAD)
    w_vb = _pad_lanes(w_kvb[:, :, MLA_NOPE:], LANE).reshape(MLA_KV_LORA, MLA_PAD)
    return {
        "g_mix": mix_norm[l][None],
        "w_in": w_r,
        "w_gate": w[:, o_gl:].astype(BF16),
        "g_dq": dil_q_norm[l].reshape(1, DIL_WIDTH),
        "g_dk": dil_k_norm[l].reshape(1, DIL_WIDTH),
        "g_qa": mla_q_a_norm[l][None],
        "g_kva": mla_kv_a_norm[l][None],
        "w_qb": w_qb.reshape(MLA_Q_LORA, MLA_PAD).astype(BF16),
        "w_kvb": jnp.concatenate([w_kb, w_vb], axis=1).astype(BF16),
        "g_mq": _pad_lanes(mla_q_norm[l][None], LANE),
        "g_mk": _pad_lanes(mla_k_norm[l][None], LANE),
        "g_memq": jnp.tile(mem_q_norm[l][None], (1, MEM_HEADS)),
        "g_mem": mem_norm[l][None],
        "w_memkv": w_mem_kv[l].astype(BF16),
        "g_memk": mem_k_norm[l][None],
        "w_branch": w_branch[l].astype(BF16),
        "w_out": w_out[l].astype(BF16),
        "g_ffn": ffn_norm[l][None],
        "w_ff1": w_ff1[l].astype(BF16),
        "w_ff2": w_ff2[l].astype(BF16),
    }


def _encoder_layer(x, mem, p):
    batch, seq, _ = x.shape
    xf = x.reshape(batch * seq, D_MODEL)
    outs = _in_proj(xf, p, batch, seq)
    dil_qkv, (qm, km, vm, mq) = outs[:3 * N_DIL], outs[3 * N_DIL:]
    a = _dilated(dil_qkv, batch, seq)
    b = _mla(qm, km, vm, batch, seq)
    kmem, vmem = _mem_kv(mem, p)
    m = _mem_attn(mq, kmem, vmem, batch, seq)
    return _post(xf, a, b, m, p).reshape(batch, seq, D_MODEL)


def kernel(x_prompt, x_sample, mem_prompt, mem_sample, mix_norm, w_in, dil_q_norm, dil_k_norm, mla_q_a_norm, mla_kv_a_norm, w_mla_q_b, w_mla_kv_b, mla_q_norm, mla_k_norm, mem_norm, w_mem_kv, mem_q_norm, mem_k_norm, w_branch, w_out, ffn_norm, w_ff1, w_ff2):
    weights = (mix_norm, w_in, dil_q_norm, dil_k_norm, mla_q_a_norm, mla_kv_a_norm, w_mla_q_b,
               w_mla_kv_b, mla_q_norm, mla_k_norm, mem_norm, w_mem_kv, mem_q_norm, mem_k_norm,
               w_branch, w_out, ffn_norm, w_ff1, w_ff2)
    yp, ys = x_prompt, x_sample
    rope_p = _rope_tables(yp.shape[1])
    rope_s = _rope_tables(ys.shape[1])
    for l in range(mix_norm.shape[0]):
        p = _prep_layer(l, *weights)
        names = ("rope_c", "rope_s1", "rope_s2")
        yp = _encoder_layer(yp, mem_prompt, {**p, **dict(zip(names, rope_p))})
        ys = _encoder_layer(ys, mem_sample, {**p, **dict(zip(names, rope_s))})
    return (yp, ys)
```

```python
import functools
import math

import numpy as np
import jax
import jax.numpy as jnp
from jax import lax
from jax.experimental import pallas as pl
from jax.experimental.pallas import tpu as pltpu

F32 = jnp.float32
BF16 = jnp.bfloat16

LANE = 128
VMEM_LIMIT_BYTES = 56 * 1024 * 1024

D_MODEL = 1024
EPS = 1e-6
NEG_INF = -1e30
LOG2E = math.log2(math.e)
DIL_GROUPS = ((128, 1), (512, 4), (2048, 16))
N_DIL = 3
DIL_HEADS = 4
DIL_HEAD_DIM = 128
DIL_WIDTH = N_DIL * DIL_HEADS * DIL_HEAD_DIM
DIL_OUT = DIL_HEADS * DIL_HEAD_DIM
MLA_HEADS = 8
MLA_Q_LORA = 384
MLA_KV_LORA = 256
MLA_NOPE = 64
MLA_ROPE = 32
MLA_QK = MLA_NOPE + MLA_ROPE
MLA_V = 64
MLA_PAD = MLA_HEADS * LANE
ROPE_THETA = 10000.0
MEM_HEADS = 4
MEM_HEAD_DIM = 128
MEM_WIDTH = MEM_HEADS * MEM_HEAD_DIM
N_BRANCH = 3
BRANCH_WIDTH = 512
D_FF = 4 * D_MODEL
GATE_WIDTH = N_BRANCH * D_MODEL

COL_DQ = 0
COL_DK = COL_DQ + DIL_WIDTH
COL_DV = COL_DK + DIL_WIDTH
COL_CQ = COL_DV + DIL_WIDTH
COL_CKV = COL_CQ + MLA_Q_LORA
COL_KR = COL_CKV + MLA_KV_LORA
COL_MQ = COL_KR + LANE
COL_END = COL_MQ + MEM_WIDTH

DIL_R = 64
DIL_SUB = 128
DIL_TOKENS = 1024
DIL_BATCH_ROWS = 256
DIL_QSCALE = LOG2E / math.sqrt(DIL_HEAD_DIM)
MLA_QSCALE = LOG2E / math.sqrt(MLA_QK)
MEM_QSCALE = LOG2E / math.sqrt(MEM_HEAD_DIM)

IN_TILE = 512
IN_SUB = 256
POST_TILE = 512
MLA_TQ = 256
MLA_TK = 512
MLA_GROUP = 8
MEM_TQ = 512
FF_CHUNK = 1024

NT_DIMS = (((1,), (1,)), ((), ()))
TN_DIMS = (((0,), (0,)), ((), ()))


def _alibi_slopes():
    n = N_DIL * DIL_HEADS
    s = np.array([2.0 ** (-8.0 * (k + 1) / n) for k in range(n)], dtype=np.float32)
    return s.reshape(N_DIL, DIL_HEADS)


def _rinv(x, width):
    return lax.rsqrt(jnp.sum(x * x, axis=-1, keepdims=True) * (1.0 / width) + EPS)


def _resident(shape):
    nd = len(shape)
    return pl.BlockSpec(shape, lambda *_: (0,) * nd, pipeline_mode=pl.Buffered(1))


def _params(semantics):
    return pltpu.CompilerParams(dimension_semantics=semantics, vmem_limit_bytes=VMEM_LIMIT_BYTES)


def _rope(t, c, s1, s2):
    return t * c + pltpu.roll(t, LANE - MLA_ROPE // 2, 1) * s1 + pltpu.roll(t, MLA_ROPE // 2, 1) * s2


def _in_kernel(x_ref, gmix_ref, w_ref, gdq_ref, gdk_ref, gqa_ref, gkva_ref, wqb_ref, wkvb_ref,
               gmq_ref, gmk_ref, gmemq_ref, rc_ref, rs1_ref, rs2_ref, *refs):
    for sub in range(x_ref.shape[0] // IN_SUB):
        _in_rows(sub, x_ref, gmix_ref, w_ref, gdq_ref, gdk_ref, gqa_ref, gkva_ref, wqb_ref, wkvb_ref,
                 gmq_ref, gmk_ref, gmemq_ref, rc_ref, rs1_ref, rs2_ref, *refs)


def _in_rows(sub, x_ref, gmix_ref, w_ref, gdq_ref, gdk_ref, gqa_ref, gkva_ref, wqb_ref, wkvb_ref,
             gmq_ref, gmk_ref, gmemq_ref, rc_ref, rs1_ref, rs2_ref, *refs):
    dil_refs, (qm_ref, km_ref, vm_ref, mq_ref) = refs[:3 * N_DIL], refs[3 * N_DIL:3 * N_DIL + 4]
    stage_refs = refs[3 * N_DIL + 4:]
    tm = IN_SUB
    rows = slice(sub * tm, (sub + 1) * tm)
    x = x_ref[rows, :]
    h = (x * _rinv(x, D_MODEL) * gmix_ref[...]).astype(BF16)

    def proj(lo, hi):
        return jnp.dot(h, w_ref[:, lo:hi], preferred_element_type=F32)

    def head_norm(blk, gain, scale):
        return blk * _rinv(blk, LANE) * gain * scale

    def emit_dilated(z, t, gains_ref, scale):
        for g, (_, dilation) in enumerate(DIL_GROUPS):
            out_ref = dil_refs[t * N_DIL + g]
            stage = None if dilation == 1 else stage_refs[t * (N_DIL - 1) + g - 1]
            for hd in range(DIL_HEADS):
                col = (g * DIL_HEADS + hd) * LANE
                blk = z[:, col:col + LANE]
                if gains_ref is not None:
                    blk = head_norm(blk, gains_ref[:, col:col + LANE], scale)
                if dilation == 1:
                    out_ref[rows, hd * LANE:(hd + 1) * LANE] = blk.astype(BF16)
                else:
                    stage[hd] = blk
            if dilation > 1:
                per = tm // dilation
                for r in range(dilation):
                    for hd in range(DIL_HEADS):
                        picked = stage[hd, pl.ds(r, per, stride=dilation), :]
                        out_ref[r, sub * per:(sub + 1) * per, hd * LANE:(hd + 1) * LANE] = picked.astype(BF16)

    rc, rs1, rs2 = rc_ref[rows, :], rs1_ref[rows, :], rs2_ref[rows, :]
    cq = proj(COL_CQ, COL_CKV)
    ckv = proj(COL_CKV, COL_KR)
    k_rope = proj(COL_KR, COL_MQ)
    cq = (cq * _rinv(cq, MLA_Q_LORA) * gqa_ref[...]).astype(BF16)
    ckv = (ckv * _rinv(ckv, MLA_KV_LORA) * gkva_ref[...]).astype(BF16)
    q = jnp.dot(cq, wqb_ref[...], preferred_element_type=F32)
    kv = jnp.dot(ckv, wkvb_ref[...], preferred_element_type=F32)
    ones_lane = (lax.broadcasted_iota(jnp.int32, (1, LANE), 1) == MLA_V).astype(F32)

    def latent_heads(lo, hi):
        for hd in range(lo, hi):
            sl = slice(hd * LANE, (hd + 1) * LANE)
            qh = q[:, sl]
            qh = qh * _rinv(qh, MLA_QK) * gmq_ref[...]
            qm_ref[rows, sl] = (_rope(qh, rc, rs1, rs2) * MLA_QSCALE).astype(BF16)
            kh = kv[:, sl] + k_rope
            kh = kh * _rinv(kh, MLA_QK) * gmk_ref[...]
            km_ref[rows, sl] = _rope(kh, rc, rs1, rs2).astype(BF16)
            vm_ref[rows, sl] = (kv[:, MLA_PAD + hd * LANE:MLA_PAD + (hd + 1) * LANE] + ones_lane).astype(BF16)

    third = MLA_HEADS // 3
    emit_dilated(proj(COL_DQ, COL_DK), 0, gdq_ref, DIL_QSCALE)
    latent_heads(0, third)
    emit_dilated(proj(COL_DK, COL_DV), 1, gdk_ref, 1.0)
    latent_heads(third, 2 * third)
    zmq = proj(COL_MQ, COL_END)
    for hd in range(MEM_HEADS):
        sl = slice(hd * LANE, (hd + 1) * LANE)
        mq_ref[rows, sl] = head_norm(zmq[:, sl], gmemq_ref[:, sl], MEM_QSCALE).astype(BF16)
    latent_heads(2 * third, MLA_HEADS)
    emit_dilated(proj(COL_DV, COL_CQ), 2, None, 1.0)


def _in_proj(xf, p, batch, seq):
    tokens = xf.shape[0]
    tm = IN_TILE
    pos_blocks = seq // tm
    row = lambda w: pl.BlockSpec((tm, w), lambda i: (i, 0))
    rope_spec = pl.BlockSpec((tm, LANE), lambda i: (i % pos_blocks, 0))
    out_specs, out_shape, scratch = [], [], []
    for _ in range(3):
        for _, dilation in DIL_GROUPS:
            if dilation == 1:
                out_specs.append(row(DIL_OUT))
                out_shape.append(jax.ShapeDtypeStruct((tokens, DIL_OUT), BF16))
            else:
                assert IN_SUB % (dilation * 16) == 0
                out_specs.append(pl.BlockSpec((None, dilation, tm // dilation, DIL_OUT),
                                              lambda i: (i // pos_blocks, 0, i % pos_blocks, 0)))
                out_shape.append(jax.ShapeDtypeStruct((batch, dilation, seq // dilation, DIL_OUT), BF16))
                scratch.append(pltpu.VMEM((DIL_HEADS, IN_SUB, LANE), F32))
    for w in (MLA_PAD, MLA_PAD, MLA_PAD, MEM_WIDTH):
        out_specs.append(row(w))
        out_shape.append(jax.ShapeDtypeStruct((tokens, w), BF16))
    return pl.pallas_call(
        _in_kernel,
        grid=(tokens // tm,),
        in_specs=[row(D_MODEL), _resident((1, D_MODEL)), _resident((D_MODEL, COL_END)),
                  _resident((1, DIL_WIDTH)), _resident((1, DIL_WIDTH)),
                  _resident((1, MLA_Q_LORA)), _resident((1, MLA_KV_LORA)),
                  _resident((MLA_Q_LORA, MLA_PAD)), _resident((MLA_KV_LORA, 2 * MLA_PAD)),
                  _resident((1, LANE)), _resident((1, LANE)), _resident((1, MEM_WIDTH)),
                  rope_spec, rope_spec, rope_spec],
        out_specs=out_specs,
        out_shape=out_shape,
        scratch_shapes=scratch,
        compiler_params=_params(("parallel",)),
        name="in_proj",
    )(xf, p["g_mix"], p["w_in"], p["g_dq"], p["g_dk"], p["g_qa"], p["g_kva"], p["w_qb"], p["w_kvb"],
      p["g_mq"], p["g_mk"], p["g_memq"], p["rope_c"], p["rope_s1"], p["rope_s2"])


def _dil_group(qkv_refs, state, o_ref, *, first, slopes, dilation, sub_len, tl, sub):
    q_ref, kp_ref, kc_ref, kn_ref, vp_ref, vc_ref, vn_ref = qkv_refs
    acc_st, m_st, d_st = state
    last = o_ref is not None

    def window(prev_ref, cur_ref, next_ref, r, r0, sl):
        lo, hi = r0 - DIL_R, r0 + sub + DIL_R
        parts = [prev_ref[r, :, sl]] if lo < 0 else []
        parts.append(cur_ref[r, max(lo, 0):min(hi, tl), sl])
        if hi > tl:
            parts.append(next_ref[r, :, sl])
        return parts[0] if len(parts) == 1 else jnp.concatenate(parts, axis=0)

    nkeys = sub + 2 * DIL_R
    qi = lax.broadcasted_iota(jnp.int32, (sub, nkeys), 0)
    kj = lax.broadcasted_iota(jnp.int32, (sub, nkeys), 1)
    absrel = jnp.abs(kj - DIL_R - qi)
    band = absrel <= DIL_R
    dist = absrel.astype(F32)
    block_start = pl.program_id(1) * tl
    n_sub = tl // sub
    bias_cache = {}

    def masked_bias(sb, hd):
        edge = sb == 0 or sb == n_sub - 1
        key = (sb if edge else -1, hd)
        if key not in bias_cache:
            ok = band
            if edge:
                upos = kj + (block_start + (sb * sub - DIL_R))
                ok = band & (upos >= 0) & (upos < sub_len)
            bias_cache[key] = jnp.where(ok, dist * (-float(slopes[hd]) * dilation * LOG2E), NEG_INF)
        return bias_cache[key]

    units = [(r, sb) for r in range(dilation) for sb in range(tl // sub)]
    batch = max(1, DIL_BATCH_ROWS // sub)
    cols = lambda hd: slice(hd * LANE, (hd + 1) * LANE)

    def score_stage(blocks):
        out = []
        for r, sb, hd in blocks:
            r0 = sb * sub
            s = lax.dot_general(q_ref[r, r0:r0 + sub, cols(hd)], window(kp_ref, kc_ref, kn_ref, r, r0, cols(hd)),
                                NT_DIMS, preferred_element_type=F32)
            out.append(s + masked_bias(sb, hd))
        return out

    def softmax_stage(blocks, scores):
        out = []
        for s in scores:
            m = jnp.max(s, axis=-1, keepdims=True)
            pr = jnp.exp2(s - m)
            out.append((m, pr.astype(BF16)))
        return out

    def value_stage(blocks, stats):
        out = []
        ones = jnp.ones((nkeys, LANE), BF16)
        for (r, sb, hd), (m, pr) in zip(blocks, stats):
            r0 = sb * sub
            v_ext = jnp.concatenate([window(vp_ref, vc_ref, vn_ref, r, r0, cols(hd)), ones], axis=-1)
            res = jnp.dot(pr, v_ext, preferred_element_type=F32)
            out.append((res[:, :LANE], m, res[:, LANE:]))
        return out

    def merge_stage(blocks, outs):
        for (r, sb, hd), (acc, m, den) in zip(blocks, outs):
            r0 = sb * sub
            if dilation == 1:
                rows = slice(r0, r0 + sub)
            else:
                rows = pl.ds(r0 * dilation + r, sub, stride=dilation)
            if not first:
                m_prev = m_st[hd, rows, :]
                m_new = jnp.maximum(m_prev, m)
                w_prev = jnp.exp2(m_prev - m_new)
                w_cur = jnp.exp2(m - m_new)
                acc = w_prev * acc_st[hd, rows, :] + w_cur * acc
                den = w_prev * d_st[hd, rows, :] + w_cur * den
                m = m_new
            if last:
                o_ref[hd, rows, :] = acc / den
            else:
                acc_st[hd, rows, :] = acc
                m_st[hd, rows, :] = jnp.broadcast_to(m, (sub, LANE))
                d_st[hd, rows, :] = jnp.broadcast_to(den, (sub, LANE))

    stages = (score_stage, softmax_stage, value_stage, merge_stage)
    return [(stages, [(r, sb, hd) for r, sb in units[u0:u0 + batch] for hd in range(DIL_HEADS)])
            for u0 in range(0, len(units), batch)]


def _dil_geometry(g, seq):
    window, dilation = DIL_GROUPS[g]
    assert window // (2 * dilation) == DIL_R
    sub_len = seq // dilation
    assert sub_len * dilation == seq and sub_len % DIL_R == 0
    tl = min(sub_len, DIL_TOKENS // dilation)
    sub = min(tl, DIL_SUB)
    assert sub_len % tl == 0 and tl % sub == 0 and tl % DIL_R == 0
    return dict(slopes=_alibi_slopes()[g], dilation=dilation, sub_len=sub_len, tl=tl, sub=sub)


def _dil_kernel(*refs, geometry):
    n_in = 7 * N_DIL
    o_ref = refs[n_in]
    state = refs[n_in + 1:]
    work = []
    for g, geo in enumerate(geometry):
        work += _dil_group(refs[7 * g:7 * g + 7], state, o_ref if g == N_DIL - 1 else None, first=g == 0, **geo)
    n_stage = 4
    carried = [None] * len(work)
    for t in range(len(work) + n_stage - 1):
        for s in reversed(range(n_stage)):
            i = t - s
            if 0 <= i < len(work):
                stages, blocks = work[i]
                carried[i] = stages[s](blocks) if s == 0 else stages[s](blocks, carried[i])


def _dilated(dil_qkv, batch, seq):
    block = min(seq, DIL_TOKENS)
    geometry = [_dil_geometry(g, seq) for g in range(N_DIL)]
    args, in_specs = [], []
    for g, geo in enumerate(geometry):
        dilation, tl, sub_len = geo["dilation"], geo["tl"], geo["sub_len"]
        assert tl * dilation == block
        edge_blocks = sub_len // DIL_R
        per_tl = tl // DIL_R
        cur_spec = pl.BlockSpec((None, dilation, tl, DIL_OUT), lambda b, i: (b, 0, i, 0))
        prev_spec = pl.BlockSpec((None, dilation, DIL_R, DIL_OUT),
                                 lambda b, i, per_tl=per_tl: (b, 0, jnp.maximum(i * per_tl - 1, 0), 0))
        next_spec = pl.BlockSpec(
            (None, dilation, DIL_R, DIL_OUT),
            lambda b, i, per_tl=per_tl, edge_blocks=edge_blocks:
                (b, 0, jnp.minimum((i + 1) * per_tl, edge_blocks - 1), 0))
        q, k, v = (dil_qkv[t * N_DIL + g].reshape(batch, dilation, sub_len, DIL_OUT) for t in range(3))
        args += [q, k, k, k, v, v, v]
        in_specs += [cur_spec, prev_spec, cur_spec, next_spec, prev_spec, cur_spec, next_spec]
    scratch = [pltpu.VMEM((DIL_HEADS, block, LANE), F32)] * 3
    return pl.pallas_call(
        functools.partial(_dil_kernel, geometry=geometry),
        grid=(batch, seq // block),
        in_specs=in_specs,
        out_specs=pl.BlockSpec((None, DIL_HEADS, block, LANE), lambda b, i: (b, 0, i, 0)),
        out_shape=jax.ShapeDtypeStruct((batch, DIL_HEADS, seq, LANE), F32),
        scratch_shapes=scratch,
        compiler_params=_params(("parallel", "parallel")),
        name="dilated_attn",
    )(*args)


def _mla_kernel(q_ref, k_ref, v_ref, o_ref, s_buf, p_buf, *, tq, tk, nk):
    assert nk >= 2 and nk % 2 == 0
    for grp in range(MLA_HEADS // MLA_GROUP):
        cols = [slice(hd * LANE, (hd + 1) * LANE) for hd in range(grp * MLA_GROUP, (grp + 1) * MLA_GROUP)]
        qs = [q_ref[:, sl] for sl in cols]

        def tile(j):
            return slice(j * tk, (j + 1) * tk)

        def scores(j, h, slot):
            s = lax.dot_general(k_ref[tile(j), cols[h]], qs[h], NT_DIMS, preferred_element_type=F32)
            s_buf[slot, h] = s
            return jnp.max(s, axis=0, keepdims=True)

        def numerators(h, slot, m, tile_max):
            m_new = jnp.maximum(m, tile_max)
            p_buf[slot, h] = jnp.exp2(s_buf[slot, h] - m_new).astype(BF16)
            return m_new, jnp.exp2(m - m_new)

        def accumulate(j, h, slot, alpha, acc):
            pv = lax.dot_general(v_ref[tile(j), cols[h]], p_buf[slot, h], TN_DIMS,
                                 preferred_element_type=F32)
            return alpha * acc + pv

        def trip(j, slot, state):
            out = []
            for h, (m, alpha, tile_max, acc) in enumerate(state):
                acc = accumulate(j - 1, h, 1 - slot, alpha, acc)
                m, alpha = numerators(h, slot, m, tile_max)
                out.append((m, alpha, scores(j + 1, h, 1 - slot), acc))
            return tuple(out)

        state = []
        for h in range(MLA_GROUP):
            m0 = jnp.full((1, tq), -jnp.inf, F32)
            m, alpha = numerators(h, 0, m0, scores(0, h, 0))
            state.append((m, alpha, scores(1, h, 1), jnp.zeros((LANE, tq), F32)))

        for j in range(1, nk - 1):
            state = trip(j, j % 2, state)
        last = (nk - 1) % 2
        tops = []
        for h, (m, alpha, tile_max, acc) in enumerate(state):
            acc = accumulate(nk - 2, h, 1 - last, alpha, acc)
            m, alpha = numerators(h, last, m, tile_max)
            acc = accumulate(nk - 1, h, last, alpha, acc)
            tops.append((acc / acc[MLA_V:MLA_V + 1, :])[:MLA_V])
        for i in range(MLA_GROUP // 2):
            col = (grp * MLA_GROUP // 2 + i) * LANE
            o_ref[:, col:col + LANE] = jnp.concatenate(tops[2 * i:2 * i + 2], axis=0).T.astype(BF16)


def _mla(qm, km, vm, batch, seq):
    tq, tk = MLA_TQ, MLA_TK
    view = lambda t: t.reshape(batch, seq, MLA_PAD)
    kv_spec = pl.BlockSpec((None, seq, MLA_PAD), lambda b, i: (b, 0, 0))
    out = pl.pallas_call(
        functools.partial(_mla_kernel, tq=tq, tk=tk, nk=seq // tk),
        grid=(batch, seq // tq),
        in_specs=[pl.BlockSpec((None, tq, MLA_PAD), lambda b, i: (b, i, 0)), kv_spec, kv_spec],
        out_specs=pl.BlockSpec((None, tq, MLA_HEADS * MLA_V), lambda b, i: (b, i, 0)),
        out_shape=jax.ShapeDtypeStruct((batch, seq, MLA_HEADS * MLA_V), BF16),
        scratch_shapes=[pltpu.VMEM((2, MLA_GROUP, tk, tq), F32), pltpu.VMEM((2, MLA_GROUP, tk, tq), BF16)],
        compiler_params=_params(("parallel", "arbitrary")),
        name="latent_attn",
    )(view(qm), view(km), view(vm))
    return out.reshape(batch * seq, MLA_HEADS * MLA_V)


def _memkv_kernel(mem_ref, g_ref, w_ref, gk_ref, k_ref, v_ref):
    x = mem_ref[...]
    h = (x * _rinv(x, D_MODEL) * g_ref[...]).astype(BF16)
    kv = jnp.dot(h, w_ref[...], preferred_element_type=F32)
    for hd in range(MEM_HEADS):
        sl = slice(hd * LANE, (hd + 1) * LANE)
        blk = kv[:, sl]
        k_ref[:, sl] = (blk * _rinv(blk, MEM_HEAD_DIM) * gk_ref[...]).astype(BF16)
    v_ref[...] = kv[:, MEM_WIDTH:].astype(BF16)


def _mem_kv(mem, p):
    batch, n_mem, _ = mem.shape
    spec = lambda w: pl.BlockSpec((None, n_mem, w), lambda b: (b, 0, 0))
    return pl.pallas_call(
        _memkv_kernel,
        grid=(batch,),
        in_specs=[spec(D_MODEL), _resident((1, D_MODEL)), _resident((D_MODEL, 2 * MEM_WIDTH)),
                  _resident((1, LANE))],
        out_specs=[spec(MEM_WIDTH), spec(MEM_WIDTH)],
        out_shape=[jax.ShapeDtypeStruct((batch, n_mem, MEM_WIDTH), BF16)] * 2,
        compiler_params=_params(("parallel",)),
        name="memory_kv",
    )(mem, p["g_mem"], p["w_memkv"], p["g_memk"])


def _memattn_kernel(q_ref, k_ref, v_ref, o_ref):
    for hd in range(MEM_HEADS):
        sl = slice(hd * LANE, (hd + 1) * LANE)
        s = lax.dot_general(q_ref[:, sl], k_ref[:, sl], NT_DIMS, preferred_element_type=F32)
        m = jnp.max(s, axis=-1, keepdims=True)
        pr = jnp.exp2(s - m)
        den = jnp.sum(pr, axis=-1, keepdims=True)
        o = jnp.dot(pr.astype(BF16), v_ref[:, sl], preferred_element_type=F32) / den
        o_ref[:, sl] = o.astype(BF16)


def _mem_attn(mq, kmem, vmem, batch, seq):
    n_mem = kmem.shape[1]
    tq = MEM_TQ
    kv_spec = pl.BlockSpec((None, n_mem, MEM_WIDTH), lambda b, i: (b, 0, 0))
    q_spec = pl.BlockSpec((None, tq, MEM_WIDTH), lambda b, i: (b, i, 0))
    out = pl.pallas_call(
        _memattn_kernel,
        grid=(batch, seq // tq),
        in_specs=[q_spec, kv_spec, kv_spec],
        out_specs=q_spec,
        out_shape=jax.ShapeDtypeStruct((batch, seq, MEM_WIDTH), BF16),
        compiler_params=_params(("parallel", "parallel")),
        name="memory_attn",
    )(mq.reshape(batch, seq, MEM_WIDTH), kmem, vmem)
    return out.reshape(batch * seq, MEM_WIDTH)


def _post_kernel(x_ref, a_ref, b_ref, m_ref, gmix_ref, wg_ref, wb_ref, wo_ref, gffn_ref, w1_ref, w2_ref,
                 o_ref):
    x = x_ref[...]
    h = (x * _rinv(x, D_MODEL) * gmix_ref[...]).astype(BF16)
    a = jnp.concatenate([a_ref[hd] for hd in range(DIL_HEADS)], axis=-1).astype(BF16)
    mixed = None
    for i, br in enumerate((a, b_ref[...], m_ref[...])):
        proj = jnp.dot(br, wb_ref[i], preferred_element_type=F32)
        logits = jnp.dot(h, wg_ref[:, i * D_MODEL:(i + 1) * D_MODEL], preferred_element_type=F32)
        gate = jax.nn.sigmoid(logits)
        mixed = gate * proj if mixed is None else mixed + gate * proj
    x1 = x + jnp.dot(mixed.astype(BF16), wo_ref[...], preferred_element_type=F32)
    h2 = (x1 * _rinv(x1, D_MODEL) * gffn_ref[...]).astype(BF16)
    acc = x1
    for c in range(D_FF // FF_CHUNK):
        sl = slice(c * FF_CHUNK, (c + 1) * FF_CHUNK)
        u = jnp.dot(h2, w1_ref[:, sl], preferred_element_type=F32)
        u = jnp.square(jnp.maximum(u, 0.0)).astype(BF16)
        acc = acc + jnp.dot(u, w2_ref[sl, :], preferred_element_type=F32)
    o_ref[...] = acc


def _post(xf, a, b, m, p):
    tokens = xf.shape[0]
    tm = POST_TILE
    row = lambda w: pl.BlockSpec((tm, w), lambda i: (i, 0))
    seq_blocks = a.shape[2] // tm
    a_spec = pl.BlockSpec((None, DIL_HEADS, tm, LANE), lambda i: (i // seq_blocks, 0, i % seq_blocks, 0))
    return pl.pallas_call(
        _post_kernel,
        grid=(tokens // tm,),
        in_specs=[row(D_MODEL), a_spec, row(BRANCH_WIDTH), row(BRANCH_WIDTH),
                  _resident((1, D_MODEL)), _resident((D_MODEL, GATE_WIDTH)),
                  _resident((N_BRANCH, BRANCH_WIDTH, D_MODEL)), _resident((D_MODEL, D_MODEL)),
                  _resident((1, D_MODEL)), _resident((D_MODEL, D_FF)), _resident((D_FF, D_MODEL))],
        out_specs=row(D_MODEL),
        out_shape=jax.ShapeDtypeStruct((tokens, D_MODEL), F32),
        compiler_params=_params(("parallel",)),
        name="merge_ffn",
    )(xf, a, b, m, p["g_mix"], p["w_gate"], p["w_branch"], p["w_out"], p["g_ffn"], p["w_ff1"], p["w_ff2"])


def _rope_tables(seq):
    half = MLA_ROPE // 2
    inv = ROPE_THETA ** (-jnp.arange(half, dtype=F32) * 2.0 / MLA_ROPE)
    ang = jnp.arange(seq, dtype=F32)[:, None] * inv[None, :]
    c, s = jnp.cos(ang), jnp.sin(ang)
    z = lambda w: jnp.zeros((seq, w), F32)
    tail = LANE - MLA_QK
    rope_c = jnp.concatenate([jnp.ones((seq, MLA_NOPE), F32), c, c, z(tail)], axis=-1)
    rope_s1 = jnp.concatenate([z(MLA_NOPE), -s, z(half), z(tail)], axis=-1)
    rope_s2 = jnp.concatenate([z(MLA_NOPE), z(half), s, z(tail)], axis=-1)
    return rope_c, rope_s1, rope_s2


def _pad_lanes(t, width):
    return jnp.pad(t, [(0, 0)] * (t.ndim - 1) + [(0, width - t.shape[-1])])


def _prep_layer(l, mix_norm, w_in, dil_q_norm, dil_k_norm, mla_q_a_norm, mla_kv_a_norm, w_mla_q_b,
                w_mla_kv_b, mla_q_norm, mla_k_norm, mem_norm, w_mem_kv, mem_q_norm, mem_k_norm,
                w_branch, w_out, ffn_norm, w_ff1, w_ff2):
    w = w_in[l]
    o_cq = 3 * DIL_WIDTH
    o_ckv = o_cq + MLA_Q_LORA
    o_kr = o_ckv + MLA_KV_LORA
    o_mq = o_kr + MLA_ROPE
    o_gl = o_mq + MEM_WIDTH
    kr = jnp.pad(w[:, o_kr:o_mq], ((0, 0), (MLA_NOPE, LANE - MLA_QK)))
    w_r = jnp.concatenate([w[:, :o_kr], kr, w[:, o_mq:o_gl]], axis=1).astype(BF16)
    w_qb = _pad_lanes(w_mla_q_b[l].reshape(MLA_Q_LORA, MLA_HEADS, MLA_QK), LANE)
    w_kvb = w_mla_kv_b[l].reshape(MLA_KV_LORA, MLA_HEADS, MLA_NOPE + MLA_V)
    w_kb = _pad_lanes(w_kvb[:, :, :MLA_NOPE], LANE).reshape(MLA_KV_LORA, MLA_PAD)
    w_vb = _pad_lanes(w_kvb[:, :, MLA_NOPE:], LANE).reshape(MLA_KV_LORA, MLA_PAD)
    return {
        "g_mix": mix_norm[l][None],
        "w_in": w_r,
        "w_gate": w[:, o_gl:].astype(BF16),
        "g_dq": dil_q_norm[l].reshape(1, DIL_WIDTH),
        "g_dk": dil_k_norm[l].reshape(1, DIL_WIDTH),
        "g_qa": mla_q_a_norm[l][None],
        "g_kva": mla_kv_a_norm[l][None],
        "w_qb": w_qb.reshape(MLA_Q_LORA, MLA_PAD).astype(BF16),
        "w_kvb": jnp.concatenate([w_kb, w_vb], axis=1).astype(BF16),
        "g_mq": _pad_lanes(mla_q_norm[l][None], LANE),
        "g_mk": _pad_lanes(mla_k_norm[l][None], LANE),
        "g_memq": jnp.tile(mem_q_norm[l][None], (1, MEM_HEADS)),
        "g_mem": mem_norm[l][None],
        "w_memkv": w_mem_kv[l].astype(BF16),
        "g_memk": mem_k_norm[l][None],
        "w_branch": w_branch[l].astype(BF16),
        "w_out": w_out[l].astype(BF16),
        "g_ffn": ffn_norm[l][None],
        "w_ff1": w_ff1[l].astype(BF16),
        "w_ff2": w_ff2[l].astype(BF16),
    }


def _encoder_layer(x, mem, p):
    batch, seq, _ = x.shape
    xf = x.reshape(batch * seq, D_MODEL)
    outs = _in_proj(xf, p, batch, seq)
    dil_qkv, (qm, km, vm, mq) = outs[:3 * N_DIL], outs[3 * N_DIL:]
    a = _dilated(dil_qkv, batch, seq)
    b = _mla(qm, km, vm, batch, seq)
    kmem, vmem = _mem_kv(mem, p)
    m = _mem_attn(mq, kmem, vmem, batch, seq)
    return _post(xf, a, b, m, p).reshape(batch, seq, D_MODEL)


def kernel(x_prompt, x_sample, mem_prompt, mem_sample, mix_norm, w_in, dil_q_norm, dil_k_norm, mla_q_a_norm, mla_kv_a_norm, w_mla_q_b, w_mla_kv_b, mla_q_norm, mla_k_norm, mem_norm, w_mem_kv, mem_q_norm, mem_k_norm, w_branch, w_out, ffn_norm, w_ff1, w_ff2):
    weights = (mix_norm, w_in, dil_q_norm, dil_k_norm, mla_q_a_norm, mla_kv_a_norm, w_mla_q_b,
               w_mla_kv_b, mla_q_norm, mla_k_norm, mem_norm, w_mem_kv, mem_q_norm, mem_k_norm,
               w_branch, w_out, ffn_norm, w_ff1, w_ff2)
    yp, ys = x_prompt, x_sample
    rope_p = _rope_tables(yp.shape[1])
    rope_s = _rope_tables(ys.shape[1])
    for l in range(mix_norm.shape[0]):
        p = _prep_layer(l, *weights)
        names = ("rope_c", "rope_s1", "rope_s2")
        yp = _encoder_layer(yp, mem_prompt, {**p, **dict(zip(names, rope_p))})
        ys = _encoder_layer(ys, mem_sample, {**p, **dict(zip(names, rope_s))})
    return (yp, ys)
```

```python
import functools
import math

import numpy as np
import jax
import jax.numpy as jnp
from jax import lax
from jax.experimental import pallas as pl
from jax.experimental.pallas import tpu as pltpu

F32 = jnp.float32
BF16 = jnp.bfloat16

LANE = 128
VMEM_LIMIT_BYTES = 56 * 1024 * 1024

D_MODEL = 1024
EPS = 1e-6
NEG_INF = -1e30
LOG2E = math.log2(math.e)
DIL_GROUPS = ((128, 1), (512, 4), (2048, 16))
N_DIL = 3
DIL_HEADS = 4
DIL_HEAD_DIM = 128
DIL_WIDTH = N_DIL * DIL_HEADS * DIL_HEAD_DIM
DIL_OUT = DIL_HEADS * DIL_HEAD_DIM
MLA_HEADS = 8
MLA_Q_LORA = 384
MLA_KV_LORA = 256
MLA_NOPE = 64
MLA_ROPE = 32
MLA_QK = MLA_NOPE + MLA_ROPE
MLA_V = 64
MLA_PAD = MLA_HEADS * LANE
ROPE_THETA = 10000.0
MEM_HEADS = 4
MEM_HEAD_DIM = 128
MEM_WIDTH = MEM_HEADS * MEM_HEAD_DIM
N_BRANCH = 3
BRANCH_WIDTH = 512
D_FF = 4 * D_MODEL
GATE_WIDTH = N_BRANCH * D_MODEL

COL_DQ = 0
COL_DK = COL_DQ + DIL_WIDTH
COL_DV = COL_DK + DIL_WIDTH
COL_CQ = COL_DV + DIL_WIDTH
COL_CKV = COL_CQ + MLA_Q_LORA
COL_KR = COL_CKV + MLA_KV_LORA
COL_MQ = COL_KR + LANE
COL_END = COL_MQ + MEM_WIDTH

DIL_R = 64
DIL_SUB = 128
DIL_TOKENS = 1024
DIL_BATCH_ROWS = 256
DIL_QSCALE = LOG2E / math.sqrt(DIL_HEAD_DIM)
MLA_QSCALE = LOG2E / math.sqrt(MLA_QK)
MEM_QSCALE = LOG2E / math.sqrt(MEM_HEAD_DIM)

IN_TILE = 512
IN_SUB = 256
POST_TILE = 512
MLA_TQ = 256
MLA_QBLOCK = 512
MLA_TK = 512
MLA_GROUP = 8
FF_CHUNK = 1024

NT_DIMS = (((1,), (1,)), ((), ()))
TN_DIMS = (((0,), (0,)), ((), ()))


def _alibi_slopes():
    n = N_DIL * DIL_HEADS
    s = np.array([2.0 ** (-8.0 * (k + 1) / n) for k in range(n)], dtype=np.float32)
    return s.reshape(N_DIL, DIL_HEADS)


def _rinv(x, width):
    return lax.rsqrt(jnp.sum(x * x, axis=-1, keepdims=True) * (1.0 / width) + EPS)


def _resident(shape):
    nd = len(shape)
    return pl.BlockSpec(shape, lambda *_: (0,) * nd, pipeline_mode=pl.Buffered(1))


def _params(semantics):
    return pltpu.CompilerParams(dimension_semantics=semantics, vmem_limit_bytes=VMEM_LIMIT_BYTES)


def _rope(t, c, s1, s2):
    return t * c + pltpu.roll(t, LANE - MLA_ROPE // 2, 1) * s1 + pltpu.roll(t, MLA_ROPE // 2, 1) * s2


def _in_kernel(x_ref, gmix_ref, w_ref, gdq_ref, gdk_ref, gqa_ref, gkva_ref, wqb_ref, wkvb_ref,
               gmq_ref, gmk_ref, gmemq_ref, rc_ref, rs1_ref, rs2_ref, *refs):
    for sub in range(x_ref.shape[0] // IN_SUB):
        _in_rows(sub, x_ref, gmix_ref, w_ref, gdq_ref, gdk_ref, gqa_ref, gkva_ref, wqb_ref, wkvb_ref,
                 gmq_ref, gmk_ref, gmemq_ref, rc_ref, rs1_ref, rs2_ref, *refs)


def _in_rows(sub, x_ref, gmix_ref, w_ref, gdq_ref, gdk_ref, gqa_ref, gkva_ref, wqb_ref, wkvb_ref,
             gmq_ref, gmk_ref, gmemq_ref, rc_ref, rs1_ref, rs2_ref, *refs):
    dil_refs, (qm_ref, km_ref, vm_ref, mq_ref) = refs[:3 * N_DIL], refs[3 * N_DIL:3 * N_DIL + 4]
    stage_refs = refs[3 * N_DIL + 4:]
    tm = IN_SUB
    rows = slice(sub * tm, (sub + 1) * tm)
    x = x_ref[rows, :]
    h = (x * _rinv(x, D_MODEL) * gmix_ref[...]).astype(BF16)

    def proj(lo, hi):
        return jnp.dot(h, w_ref[:, lo:hi], preferred_element_type=F32)

    def head_norm(blk, gain, scale):
        return blk * _rinv(blk, LANE) * gain * scale

    def emit_dilated(z, t, gains_ref, scale):
        for g, (_, dilation) in enumerate(DIL_GROUPS):
            out_ref = dil_refs[t * N_DIL + g]
            stage = None if dilation == 1 else stage_refs[t * (N_DIL - 1) + g - 1]
            for hd in range(DIL_HEADS):
                col = (g * DIL_HEADS + hd) * LANE
                blk = z[:, col:col + LANE]
                if gains_ref is not None:
                    blk = head_norm(blk, gains_ref[:, col:col + LANE], scale)
                if dilation == 1:
                    out_ref[rows, hd * LANE:(hd + 1) * LANE] = blk.astype(BF16)
                else:
                    stage[hd] = blk
            if dilation > 1:
                per = tm // dilation
                for r in range(dilation):
                    for hd in range(DIL_HEADS):
                        picked = stage[hd, pl.ds(r, per, stride=dilation), :]
                        out_ref[r, sub * per:(sub + 1) * per, hd * LANE:(hd + 1) * LANE] = picked.astype(BF16)

    rc, rs1, rs2 = rc_ref[rows, :], rs1_ref[rows, :], rs2_ref[rows, :]
    cq = proj(COL_CQ, COL_CKV)
    ckv = proj(COL_CKV, COL_KR)
    k_rope = proj(COL_KR, COL_MQ)
    cq = (cq * _rinv(cq, MLA_Q_LORA) * gqa_ref[...]).astype(BF16)
    ckv = (ckv * _rinv(ckv, MLA_KV_LORA) * gkva_ref[...]).astype(BF16)
    q = jnp.dot(cq, wqb_ref[...], preferred_element_type=F32)
    kv = jnp.dot(ckv, wkvb_ref[...], preferred_element_type=F32)
    ones_lane = (lax.broadcasted_iota(jnp.int32, (1, LANE), 1) == MLA_V).astype(F32)

    def latent_heads(lo, hi):
        for hd in range(lo, hi):
            sl = slice(hd * LANE, (hd + 1) * LANE)
            qh = q[:, sl]
            qh = qh * _rinv(qh, MLA_QK) * gmq_ref[...]
            qm_ref[rows, sl] = (_rope(qh, rc, rs1, rs2) * MLA_QSCALE).astype(BF16)
            kh = kv[:, sl] + k_rope
            kh = kh * _rinv(kh, MLA_QK) * gmk_ref[...]
            km_ref[rows, sl] = _rope(kh, rc, rs1, rs2).astype(BF16)
            vm_ref[rows, sl] = (kv[:, MLA_PAD + hd * LANE:MLA_PAD + (hd + 1) * LANE] + ones_lane).astype(BF16)

    third = MLA_HEADS // 3
    emit_dilated(proj(COL_DQ, COL_DK), 0, gdq_ref, DIL_QSCALE)
    latent_heads(0, third)
    emit_dilated(proj(COL_DK, COL_DV), 1, gdk_ref, 1.0)
    latent_heads(third, 2 * third)
    zmq = proj(COL_MQ, COL_END)
    for hd in range(MEM_HEADS):
        sl = slice(hd * LANE, (hd + 1) * LANE)
        mq_ref[rows, sl] = head_norm(zmq[:, sl], gmemq_ref[:, sl], MEM_QSCALE).astype(BF16)
    latent_heads(2 * third, MLA_HEADS)
    emit_dilated(proj(COL_DV, COL_CQ), 2, None, 1.0)


def _in_proj(xf, p, batch, seq):
    tokens = xf.shape[0]
    tm = IN_TILE
    pos_blocks = seq // tm
    row = lambda w: pl.BlockSpec((tm, w), lambda i: (i, 0))
    rope_spec = pl.BlockSpec((tm, LANE), lambda i: (i % pos_blocks, 0))
    out_specs, out_shape, scratch = [], [], []
    for _ in range(3):
        for _, dilation in DIL_GROUPS:
            if dilation == 1:
                out_specs.append(row(DIL_OUT))
                out_shape.append(jax.ShapeDtypeStruct((tokens, DIL_OUT), BF16))
            else:
                assert IN_SUB % (dilation * 16) == 0
                out_specs.append(pl.BlockSpec((None, dilation, tm // dilation, DIL_OUT),
                                              lambda i: (i // pos_blocks, 0, i % pos_blocks, 0)))
                out_shape.append(jax.ShapeDtypeStruct((batch, dilation, seq // dilation, DIL_OUT), BF16))
                scratch.append(pltpu.VMEM((DIL_HEADS, IN_SUB, LANE), F32))
    for w in (MLA_PAD, MLA_PAD, MLA_PAD, MEM_WIDTH):
        out_specs.append(row(w))
        out_shape.append(jax.ShapeDtypeStruct((tokens, w), BF16))
    return pl.pallas_call(
        _in_kernel,
        grid=(tokens // tm,),
        in_specs=[row(D_MODEL), _resident((1, D_MODEL)), _resident((D_MODEL, COL_END)),
                  _resident((1, DIL_WIDTH)), _resident((1, DIL_WIDTH)),
                  _resident((1, MLA_Q_LORA)), _resident((1, MLA_KV_LORA)),
                  _resident((MLA_Q_LORA, MLA_PAD)), _resident((MLA_KV_LORA, 2 * MLA_PAD)),
                  _resident((1, LANE)), _resident((1, LANE)), _resident((1, MEM_WIDTH)),
                  rope_spec, rope_spec, rope_spec],
        out_specs=out_specs,
        out_shape=out_shape,
        scratch_shapes=scratch,
        compiler_params=_params(("parallel",)),
        name="in_proj",
    )(xf, p["g_mix"], p["w_in"], p["g_dq"], p["g_dk"], p["g_qa"], p["g_kva"], p["w_qb"], p["w_kvb"],
      p["g_mq"], p["g_mk"], p["g_memq"], p["rope_c"], p["rope_s1"], p["rope_s2"])


def _dil_group(qkv_refs, state, o_ref, *, first, slopes, dilation, sub_len, tl, sub):
    q_ref, kp_ref, kc_ref, kn_ref, vp_ref, vc_ref, vn_ref = qkv_refs
    acc_st, m_st, d_st = state
    last = o_ref is not None

    def window(prev_ref, cur_ref, next_ref, r, r0, sl):
        lo, hi = r0 - DIL_R, r0 + sub + DIL_R
        parts = [prev_ref[r, :, sl]] if lo < 0 else []
        parts.append(cur_ref[r, max(lo, 0):min(hi, tl), sl])
        if hi > tl:
            parts.append(next_ref[r, :, sl])
        return parts[0] if len(parts) == 1 else jnp.concatenate(parts, axis=0)

    nkeys = sub + 2 * DIL_R
    qi = lax.broadcasted_iota(jnp.int32, (sub, nkeys), 0)
    kj = lax.broadcasted_iota(jnp.int32, (sub, nkeys), 1)
    absrel = jnp.abs(kj - DIL_R - qi)
    band = absrel <= DIL_R
    dist = absrel.astype(F32)
    block_start = pl.program_id(1) * tl
    n_sub = tl // sub
    bias_cache = {}

    def masked_bias(sb, hd):
        edge = sb == 0 or sb == n_sub - 1
        key = (sb if edge else -1, hd)
        if key not in bias_cache:
            ok = band
            if edge:
                upos = kj + (block_start + (sb * sub - DIL_R))
                ok = band & (upos >= 0) & (upos < sub_len)
            bias_cache[key] = jnp.where(ok, dist * (-float(slopes[hd]) * dilation * LOG2E), NEG_INF)
        return bias_cache[key]

    units = [(r, sb) for r in range(dilation) for sb in range(tl // sub)]
    batch = max(1, DIL_BATCH_ROWS // sub)
    cols = lambda hd: slice(hd * LANE, (hd + 1) * LANE)

    def score_stage(blocks):
        out = []
        for r, sb, hd in blocks:
            r0 = sb * sub
            s = lax.dot_general(q_ref[r, r0:r0 + sub, cols(hd)], window(kp_ref, kc_ref, kn_ref, r, r0, cols(hd)),
                                NT_DIMS, preferred_element_type=F32)
            out.append(s + masked_bias(sb, hd))
        return out

    def softmax_stage(blocks, scores):
        out = []
        for s in scores:
            m = jnp.max(s, axis=-1, keepdims=True)
            pr = jnp.exp2(s - m)
            out.append((m, pr.astype(BF16)))
        return out

    def value_stage(blocks, stats):
        out = []
        ones = jnp.ones((nkeys, LANE), BF16)
        for (r, sb, hd), (m, pr) in zip(blocks, stats):
            r0 = sb * sub
            v_ext = jnp.concatenate([window(vp_ref, vc_ref, vn_ref, r, r0, cols(hd)), ones], axis=-1)
            res = jnp.dot(pr, v_ext, preferred_element_type=F32)
            out.append((res[:, :LANE], m, res[:, LANE:]))
        return out

    def merge_stage(blocks, outs):
        for (r, sb, hd), (acc, m, den) in zip(blocks, outs):
            r0 = sb * sub
            if dilation == 1:
                rows = slice(r0, r0 + sub)
            else:
                rows = pl.ds(r0 * dilation + r, sub, stride=dilation)
            if not first:
                m_prev = m_st[hd, rows, :]
                m_new = jnp.maximum(m_prev, m)
                w_prev = jnp.exp2(m_prev - m_new)
                w_cur = jnp.exp2(m - m_new)
                acc = w_prev * acc_st[hd, rows, :] + w_cur * acc
                den = w_prev * d_st[hd, rows, :] + w_cur * den
                m = m_new
            if last:
                o_ref[hd, rows, :] = acc / den
            else:
                acc_st[hd, rows, :] = acc
                m_st[hd, rows, :] = jnp.broadcast_to(m, (sub, LANE))
                d_st[hd, rows, :] = jnp.broadcast_to(den, (sub, LANE))

    stages = (score_stage, softmax_stage, value_stage, merge_stage)
    return [(stages, [(r, sb, hd) for r, sb in units[u0:u0 + batch] for hd in range(DIL_HEADS)])
            for u0 in range(0, len(units), batch)]


def _dil_geometry(g, seq):
    window, dilation = DIL_GROUPS[g]
    assert window // (2 * dilation) == DIL_R
    sub_len = seq // dilation
    assert sub_len * dilation == seq and sub_len % DIL_R == 0
    tl = min(sub_len, DIL_TOKENS // dilation)
    sub = min(tl, DIL_SUB)
    assert sub_len % tl == 0 and tl % sub == 0 and tl % DIL_R == 0
    return dict(slopes=_alibi_slopes()[g], dilation=dilation, sub_len=sub_len, tl=tl, sub=sub)


def _dil_kernel(*refs, geometry):
    n_in = 7 * N_DIL
    o_ref = refs[n_in]
    state = refs[n_in + 1:]
    work = []
    for g, geo in enumerate(geometry):
        work += _dil_group(refs[7 * g:7 * g + 7], state, o_ref if g == N_DIL - 1 else None, first=g == 0, **geo)
    n_stage = 4
    carried = [None] * len(work)
    for t in range(len(work) + n_stage - 1):
        for s in reversed(range(n_stage)):
            i = t - s
            if 0 <= i < len(work):
                stages, blocks = work[i]
                carried[i] = stages[s](blocks) if s == 0 else stages[s](blocks, carried[i])


def _dilated(dil_qkv, batch, seq):
    block = min(seq, DIL_TOKENS)
    geometry = [_dil_geometry(g, seq) for g in range(N_DIL)]
    args, in_specs = [], []
    for g, geo in enumerate(geometry):
        dilation, tl, sub_len = geo["dilation"], geo["tl"], geo["sub_len"]
        assert tl * dilation == block
        edge_blocks = sub_len // DIL_R
        per_tl = tl // DIL_R
        cur_spec = pl.BlockSpec((None, dilation, tl, DIL_OUT), lambda b, i: (b, 0, i, 0))
        prev_spec = pl.BlockSpec((None, dilation, DIL_R, DIL_OUT),
                                 lambda b, i, per_tl=per_tl: (b, 0, jnp.maximum(i * per_tl - 1, 0), 0))
        next_spec = pl.BlockSpec(
            (None, dilation, DIL_R, DIL_OUT),
            lambda b, i, per_tl=per_tl, edge_blocks=edge_blocks:
                (b, 0, jnp.minimum((i + 1) * per_tl, edge_blocks - 1), 0))
        q, k, v = (dil_qkv[t * N_DIL + g].reshape(batch, dilation, sub_len, DIL_OUT) for t in range(3))
        args += [q, k, k, k, v, v, v]
        in_specs += [cur_spec, prev_spec, cur_spec, next_spec, prev_spec, cur_spec, next_spec]
    scratch = [pltpu.VMEM((DIL_HEADS, block, LANE), F32)] * 3
    return pl.pallas_call(
        functools.partial(_dil_kernel, geometry=geometry),
        grid=(batch, seq // block),
        in_specs=in_specs,
        out_specs=pl.BlockSpec((None, DIL_HEADS, block, LANE), lambda b, i: (b, 0, i, 0)),
        out_shape=jax.ShapeDtypeStruct((batch, DIL_HEADS, seq, LANE), F32),
        scratch_shapes=scratch,
        compiler_params=_params(("parallel", "parallel")),
        name="dilated_attn",
    )(*args)


def _mla_kernel(q_ref, k_ref, v_ref, o_ref, s_buf, p_buf, *, tq, tk, nk):
    assert nk >= 2 and nk % 2 == 0

    def sub_tile(sub, carry):
        _mla_rows(pl.ds(pl.multiple_of(sub * tq, tq), tq), q_ref, k_ref, v_ref, o_ref, s_buf, p_buf,
                  tq=tq, tk=tk, nk=nk)
        return carry

    lax.fori_loop(0, q_ref.shape[0] // tq, sub_tile, 0)


def _mla_rows(rows, q_ref, k_ref, v_ref, o_ref, s_buf, p_buf, *, tq, tk, nk):
    for grp in range(MLA_HEADS // MLA_GROUP):
        cols = [slice(hd * LANE, (hd + 1) * LANE) for hd in range(grp * MLA_GROUP, (grp + 1) * MLA_GROUP)]
        qs = [q_ref[rows, sl] for sl in cols]

        def tile(j):
            return slice(j * tk, (j + 1) * tk)

        def scores(j, h, slot):
            s = lax.dot_general(k_ref[tile(j), cols[h]], qs[h], NT_DIMS, preferred_element_type=F32)
            s_buf[slot, h] = s
            return jnp.max(s, axis=0, keepdims=True)

        def numerators(h, slot, m, tile_max):
            m_new = jnp.maximum(m, tile_max)
            p_buf[slot, h] = jnp.exp2(s_buf[slot, h] - m_new).astype(BF16)
            return m_new, jnp.exp2(m - m_new)

        def accumulate(j, h, slot, alpha, acc):
            pv = lax.dot_general(v_ref[tile(j), cols[h]], p_buf[slot, h], TN_DIMS,
                                 preferred_element_type=F32)
            return alpha * acc + pv

        def trip(j, slot, state):
            out = []
            for h, (m, alpha, tile_max, acc) in enumerate(state):
                acc = accumulate(j - 1, h, 1 - slot, alpha, acc)
                m, alpha = numerators(h, slot, m, tile_max)
                out.append((m, alpha, scores(j + 1, h, 1 - slot), acc))
            return tuple(out)

        state = []
        for h in range(MLA_GROUP):
            m0 = jnp.full((1, tq), -jnp.inf, F32)
            m, alpha = numerators(h, 0, m0, scores(0, h, 0))
            state.append((m, alpha, scores(1, h, 1), jnp.zeros((LANE, tq), F32)))

        for j in range(1, nk - 1):
            state = trip(j, j % 2, state)
        last = (nk - 1) % 2
        tops = []
        for h, (m, alpha, tile_max, acc) in enumerate(state):
            acc = accumulate(nk - 2, h, 1 - last, alpha, acc)
            m, alpha = numerators(h, last, m, tile_max)
            acc = accumulate(nk - 1, h, last, alpha, acc)
            tops.append((acc / acc[MLA_V:MLA_V + 1, :])[:MLA_V])
        for i in range(MLA_GROUP // 2):
            col = (grp * MLA_GROUP // 2 + i) * LANE
            o_ref[rows, col:col + LANE] = jnp.concatenate(tops[2 * i:2 * i + 2], axis=0).T.astype(BF16)


def _mla(qm, km, vm, batch, seq):
    tq, tk = MLA_TQ, MLA_TK
    view = lambda t: t.reshape(batch, seq, MLA_PAD)
    kv_spec = pl.BlockSpec((None, seq, MLA_PAD), lambda b, i: (b, 0, 0))
    out = pl.pallas_call(
        functools.partial(_mla_kernel, tq=tq, tk=tk, nk=seq // tk),
        grid=(batch, seq // MLA_QBLOCK),
        in_specs=[pl.BlockSpec((None, MLA_QBLOCK, MLA_PAD), lambda b, i: (b, i, 0)), kv_spec, kv_spec],
        out_specs=pl.BlockSpec((None, MLA_QBLOCK, MLA_HEADS * MLA_V), lambda b, i: (b, i, 0)),
        out_shape=jax.ShapeDtypeStruct((batch, seq, MLA_HEADS * MLA_V), BF16),
        scratch_shapes=[pltpu.VMEM((2, MLA_GROUP, tk, tq), F32), pltpu.VMEM((2, MLA_GROUP, tk, tq), BF16)],
        compiler_params=_params(("parallel", "arbitrary")),
        name="latent_attn",
    )(view(qm), view(km), view(vm))
    return out.reshape(batch * seq, MLA_HEADS * MLA_V)


def _memkv_kernel(mem_ref, g_ref, w_ref, gk_ref, k_ref, v_ref):
    x = mem_ref[...]
    h = (x * _rinv(x, D_MODEL) * g_ref[...]).astype(BF16)
    kv = jnp.dot(h, w_ref[...], preferred_element_type=F32)
    for hd in range(MEM_HEADS):
        sl = slice(hd * LANE, (hd + 1) * LANE)
        blk = kv[:, sl]
        k_ref[:, sl] = (blk * _rinv(blk, MEM_HEAD_DIM) * gk_ref[...]).astype(BF16)
    v_ref[...] = kv[:, MEM_WIDTH:].astype(BF16)


def _mem_kv(mem, p):
    batch, n_mem, _ = mem.shape
    spec = lambda w: pl.BlockSpec((None, n_mem, w), lambda b: (b, 0, 0))
    return pl.pallas_call(
        _memkv_kernel,
        grid=(batch,),
        in_specs=[spec(D_MODEL), _resident((1, D_MODEL)), _resident((D_MODEL, 2 * MEM_WIDTH)),
                  _resident((1, LANE))],
        out_specs=[spec(MEM_WIDTH), spec(MEM_WIDTH)],
        out_shape=[jax.ShapeDtypeStruct((batch, n_mem, MEM_WIDTH), BF16)] * 2,
        compiler_params=_params(("parallel",)),
        name="memory_kv",
    )(mem, p["g_mem"], p["w_memkv"], p["g_memk"])


def _memory_attention(q, k_ref, v_ref):
    outs = []
    for hd in range(MEM_HEADS):
        sl = slice(hd * LANE, (hd + 1) * LANE)
        s = lax.dot_general(q[:, sl], k_ref[:, sl], NT_DIMS, preferred_element_type=F32)
        m = jnp.max(s, axis=-1, keepdims=True)
        pr = jnp.exp2(s - m)
        den = jnp.sum(pr, axis=-1, keepdims=True)
        o = jnp.dot(pr.astype(BF16), v_ref[:, sl], preferred_element_type=F32) / den
        outs.append(o.astype(BF16))
    return jnp.concatenate(outs, axis=-1)


def _post_kernel(x_ref, a_ref, b_ref, mq_ref, kmem_ref, vmem_ref, gmix_ref, wg_ref, wb_ref, wo_ref, gffn_ref,
                 w1_ref, w2_ref, o_ref):
    x = x_ref[...]
    h = (x * _rinv(x, D_MODEL) * gmix_ref[...]).astype(BF16)
    a = jnp.concatenate([a_ref[hd] for hd in range(DIL_HEADS)], axis=-1).astype(BF16)
    mixed = None
    m = _memory_attention(mq_ref[...], kmem_ref, vmem_ref)
    for i, br in enumerate((a, b_ref[...], m)):
        proj = jnp.dot(br, wb_ref[i], preferred_element_type=F32)
        logits = jnp.dot(h, wg_ref[:, i * D_MODEL:(i + 1) * D_MODEL], preferred_element_type=F32)
        gate = jax.nn.sigmoid(logits)
        mixed = gate * proj if mixed is None else mixed + gate * proj
    x1 = x + jnp.dot(mixed.astype(BF16), wo_ref[...], preferred_element_type=F32)
    h2 = (x1 * _rinv(x1, D_MODEL) * gffn_ref[...]).astype(BF16)
    acc = x1
    for c in range(D_FF // FF_CHUNK):
        sl = slice(c * FF_CHUNK, (c + 1) * FF_CHUNK)
        u = jnp.dot(h2, w1_ref[:, sl], preferred_element_type=F32)
        u = jnp.square(jnp.maximum(u, 0.0)).astype(BF16)
        acc = acc + jnp.dot(u, w2_ref[sl, :], preferred_element_type=F32)
    o_ref[...] = acc


def _post(xf, a, b, mq, kmem, vmem, p):
    tokens = xf.shape[0]
    tm = POST_TILE
    row = lambda w: pl.BlockSpec((tm, w), lambda i: (i, 0))
    seq_blocks = a.shape[2] // tm
    a_spec = pl.BlockSpec((None, DIL_HEADS, tm, LANE), lambda i: (i // seq_blocks, 0, i % seq_blocks, 0))
    mem_spec = pl.BlockSpec((None, kmem.shape[1], MEM_WIDTH), lambda i: (i // seq_blocks, 0, 0))
    return pl.pallas_call(
        _post_kernel,
        grid=(tokens // tm,),
        in_specs=[row(D_MODEL), a_spec, row(BRANCH_WIDTH), row(MEM_WIDTH), mem_spec, mem_spec,
                  _resident((1, D_MODEL)), _resident((D_MODEL, GATE_WIDTH)),
                  _resident((N_BRANCH, BRANCH_WIDTH, D_MODEL)), _resident((D_MODEL, D_MODEL)),
                  _resident((1, D_MODEL)), _resident((D_MODEL, D_FF)), _resident((D_FF, D_MODEL))],
        out_specs=row(D_MODEL),
        out_shape=jax.ShapeDtypeStruct((tokens, D_MODEL), F32),
        compiler_params=_params(("parallel",)),
        name="merge_ffn",
    )(xf, a, b, mq, kmem, vmem, p["g_mix"], p["w_gate"], p["w_branch"], p["w_out"], p["g_ffn"], p["w_ff1"], p["w_ff2"])


def _rope_tables(seq):
    half = MLA_ROPE // 2
    inv = ROPE_THETA ** (-jnp.arange(half, dtype=F32) * 2.0 / MLA_ROPE)
    ang = jnp.arange(seq, dtype=F32)[:, None] * inv[None, :]
    c, s = jnp.cos(ang), jnp.sin(ang)
    z = lambda w: jnp.zeros((seq, w), F32)
    tail = LANE - MLA_QK
    rope_c = jnp.concatenate([jnp.ones((seq, MLA_NOPE), F32), c, c, z(tail)], axis=-1)
    rope_s1 = jnp.concatenate([z(MLA_NOPE), -s, z(half), z(tail)], axis=-1)
    rope_s2 = jnp.concatenate([z(MLA_NOPE), z(half), s, z(tail)], axis=-1)
    return rope_c, rope_s1, rope_s2


def _pad_lanes(t, width):
    return jnp.pad(t, [(0, 0)] * (t.ndim - 1) + [(0, width - t.shape[-1])])


def _prep_layer(l, mix_norm, w_in, dil_q_norm, dil_k_norm, mla_q_a_norm, mla_kv_a_norm, w_mla_q_b,
                w_mla_kv_b, mla_q_norm, mla_k_norm, mem_norm, w_mem_kv, mem_q_norm, mem_k_norm,
                w_branch, w_out, ffn_norm, w_ff1, w_ff2):
    w = w_in[l]
    o_cq = 3 * DIL_WIDTH
    o_ckv = o_cq + MLA_Q_LORA
    o_kr = o_ckv + MLA_KV_LORA
    o_mq = o_kr + MLA_ROPE
    o_gl = o_mq + MEM_WIDTH
    kr = jnp.pad(w[:, o_kr:o_mq], ((0, 0), (MLA_NOPE, LANE - MLA_QK)))
    w_r = jnp.concatenate([w[:, :o_kr], kr, w[:, o_mq:o_gl]], axis=1).astype(BF16)
    w_qb = _pad_lanes(w_mla_q_b[l].reshape(MLA_Q_LORA, MLA_HEADS, MLA_QK), LANE)
    w_kvb = w_mla_kv_b[l].reshape(MLA_KV_LORA, MLA_HEADS, MLA_NOPE + MLA_V)
    w_kb = _pad_lanes(w_kvb[:, :, :MLA_NOPE], LANE).reshape(MLA_KV_LORA, MLA_PAD)
    w_vb = _pad_lanes(w_kvb[:, :, MLA_NOPE:], LANE).reshape(MLA_KV_LORA, MLA_PAD)
    return {
        "g_mix": mix_norm[l][None],
        "w_in": w_r,
        "w_gate": w[:, o_gl:].astype(BF16),
        "g_dq": dil_q_norm[l].reshape(1, DIL_WIDTH),
        "g_dk": dil_k_norm[l].reshape(1, DIL_WIDTH),
        "g_qa": mla_q_a_norm[l][None],
        "g_kva": mla_kv_a_norm[l][None],
        "w_qb": w_qb.reshape(MLA_Q_LORA, MLA_PAD).astype(BF16),
        "w_kvb": jnp.concatenate([w_kb, w_vb], axis=1).astype(BF16),
        "g_mq": _pad_lanes(mla_q_norm[l][None], LANE),
        "g_mk": _pad_lanes(mla_k_norm[l][None], LANE),
        "g_memq": jnp.tile(mem_q_norm[l][None], (1, MEM_HEADS)),
        "g_mem": mem_norm[l][None],
        "w_memkv": w_mem_kv[l].astype(BF16),
        "g_memk": mem_k_norm[l][None],
        "w_branch": w_branch[l].astype(BF16),
        "w_out": w_out[l].astype(BF16),
        "g_ffn": ffn_norm[l][None],
        "w_ff1": w_ff1[l].astype(BF16),
        "w_ff2": w_ff2[l].astype(BF16),
    }


def _encoder_layer(x, mem, p):
    batch, seq, _ = x.shape
    xf = x.reshape(batch * seq, D_MODEL)
    outs = _in_proj(xf, p, batch, seq)
    dil_qkv, (qm, km, vm, mq) = outs[:3 * N_DIL], outs[3 * N_DIL:]
    a = _dilated(dil_qkv, batch, seq)
    b = _mla(qm, km, vm, batch, seq)
    kmem, vmem = _mem_kv(mem, p)
    return _post(xf, a, b, mq, kmem, vmem, p).reshape(batch, seq, D_MODEL)


def kernel(x_prompt, x_sample, mem_prompt, mem_sample, mix_norm, w_in, dil_q_norm, dil_k_norm, mla_q_a_norm, mla_kv_a_norm, w_mla_q_b, w_mla_kv_b, mla_q_norm, mla_k_norm, mem_norm, w_mem_kv, mem_q_norm, mem_k_norm, w_branch, w_out, ffn_norm, w_ff1, w_ff2):
    weights = (mix_norm, w_in, dil_q_norm, dil_k_norm, mla_q_a_norm, mla_kv_a_norm, w_mla_q_b,
               w_mla_kv_b, mla_q_norm, mla_k_norm, mem_norm, w_mem_kv, mem_q_norm, mem_k_norm,
               w_branch, w_out, ffn_norm, w_ff1, w_ff2)
    yp, ys = x_prompt, x_sample
    rope_p = _rope_tables(yp.shape[1])
    rope_s = _rope_tables(ys.shape[1])
    for l in range(mix_norm.shape[0]):
        p = _prep_layer(l, *weights)
        names = ("rope_c", "rope_s1", "rope_s2")
        yp = _encoder_layer(yp, mem_prompt, {**p, **dict(zip(names, rope_p))})
        ys = _encoder_layer(ys, mem_sample, {**p, **dict(zip(names, rope_s))})
    return (yp, ys)
```

```python
import functools
import math

import numpy as np
import jax
import jax.numpy as jnp
from jax import lax
from jax.experimental import pallas as pl
from jax.experimental.pallas import tpu as pltpu

F32 = jnp.float32
BF16 = jnp.bfloat16

LANE = 128
VMEM_LIMIT_BYTES = 56 * 1024 * 1024

D_MODEL = 1024
EPS = 1e-6
NEG_INF = -1e30
LOG2E = math.log2(math.e)
DIL_GROUPS = ((128, 1), (512, 4), (2048, 16))
N_DIL = 3
DIL_HEADS = 4
DIL_HEAD_DIM = 128
DIL_WIDTH = N_DIL * DIL_HEADS * DIL_HEAD_DIM
DIL_OUT = DIL_HEADS * DIL_HEAD_DIM
MLA_HEADS = 8
MLA_Q_LORA = 384
MLA_KV_LORA = 256
MLA_NOPE = 64
MLA_ROPE = 32
MLA_QK = MLA_NOPE + MLA_ROPE
MLA_V = 64
MLA_PAD = MLA_HEADS * LANE
ROPE_THETA = 10000.0
MEM_HEADS = 4
MEM_HEAD_DIM = 128
MEM_WIDTH = MEM_HEADS * MEM_HEAD_DIM
N_BRANCH = 3
BRANCH_WIDTH = 512
D_FF = 4 * D_MODEL
GATE_WIDTH = N_BRANCH * D_MODEL

COL_DQ = 0
COL_DK = COL_DQ + DIL_WIDTH
COL_DV = COL_DK + DIL_WIDTH
COL_CQ = COL_DV + DIL_WIDTH
COL_CKV = COL_CQ + MLA_Q_LORA
COL_KR = COL_CKV + MLA_KV_LORA
COL_MQ = COL_KR + LANE
COL_END = COL_MQ + MEM_WIDTH

DIL_R = 64
DIL_SUB = 128
DIL_TOKENS = 1024
DIL_BATCH_ROWS = 256
DIL_QSCALE = LOG2E / math.sqrt(DIL_HEAD_DIM)
MLA_QSCALE = LOG2E / math.sqrt(MLA_QK)
MEM_QSCALE = LOG2E / math.sqrt(MEM_HEAD_DIM)

IN_TILE = 512
IN_SUB = 256
POST_TILE = 512
MLA_TQ = 256
MLA_QBLOCK = 512
MLA_TK = 512
MLA_GROUP = 8
FF_CHUNK = 1024

NT_DIMS = (((1,), (1,)), ((), ()))
TN_DIMS = (((0,), (0,)), ((), ()))


def _alibi_slopes():
    n = N_DIL * DIL_HEADS
    s = np.array([2.0 ** (-8.0 * (k + 1) / n) for k in range(n)], dtype=np.float32)
    return s.reshape(N_DIL, DIL_HEADS)


def _rinv(x, width):
    return lax.rsqrt(jnp.sum(x * x, axis=-1, keepdims=True) * (1.0 / width) + EPS)


def _resident(layer, shape):
    nd = len(shape)
    return pl.BlockSpec((None,) + tuple(shape), lambda *_: (layer,) + (0,) * nd, pipeline_mode=pl.Buffered(1))


def _params(semantics):
    return pltpu.CompilerParams(dimension_semantics=semantics, vmem_limit_bytes=VMEM_LIMIT_BYTES)


def _rope(t, c, s1, s2):
    return t * c + pltpu.roll(t, LANE - MLA_ROPE // 2, 1) * s1 + pltpu.roll(t, MLA_ROPE // 2, 1) * s2


def _in_kernel(x_ref, gmix_ref, w_ref, gdq_ref, gdk_ref, gqa_ref, gkva_ref, wqb_ref, wkvb_ref,
               gmq_ref, gmk_ref, gmemq_ref, rc_ref, rs1_ref, rs2_ref, *refs):
    for sub in range(x_ref.shape[0] // IN_SUB):
        _in_rows(sub, x_ref, gmix_ref, w_ref, gdq_ref, gdk_ref, gqa_ref, gkva_ref, wqb_ref, wkvb_ref,
                 gmq_ref, gmk_ref, gmemq_ref, rc_ref, rs1_ref, rs2_ref, *refs)


def _in_rows(sub, x_ref, gmix_ref, w_ref, gdq_ref, gdk_ref, gqa_ref, gkva_ref, wqb_ref, wkvb_ref,
             gmq_ref, gmk_ref, gmemq_ref, rc_ref, rs1_ref, rs2_ref, *refs):
    dil_refs, (qm_ref, km_ref, vm_ref, mq_ref) = refs[:3 * N_DIL], refs[3 * N_DIL:3 * N_DIL + 4]
    stage_refs = refs[3 * N_DIL + 4:]
    tm = IN_SUB
    rows = slice(sub * tm, (sub + 1) * tm)
    x = x_ref[rows, :]
    h = (x * _rinv(x, D_MODEL) * gmix_ref[...]).astype(BF16)

    def proj(lo, hi):
        return jnp.dot(h, w_ref[:, lo:hi], preferred_element_type=F32)

    def head_norm(blk, gain, scale):
        return blk * _rinv(blk, LANE) * gain * scale

    def emit_dilated(z, t, gains_ref, scale):
        for g, (_, dilation) in enumerate(DIL_GROUPS):
            out_ref = dil_refs[t * N_DIL + g]
            stage = None if dilation == 1 else stage_refs[t * (N_DIL - 1) + g - 1]
            for hd in range(DIL_HEADS):
                col = (g * DIL_HEADS + hd) * LANE
                blk = z[:, col:col + LANE]
                if gains_ref is not None:
                    blk = head_norm(blk, gains_ref[:, col:col + LANE], scale)
                if dilation == 1:
                    out_ref[rows, hd * LANE:(hd + 1) * LANE] = blk.astype(BF16)
                else:
                    stage[hd] = blk
            if dilation > 1:
                per = tm // dilation
                for r in range(dilation):
                    for hd in range(DIL_HEADS):
                        picked = stage[hd, pl.ds(r, per, stride=dilation), :]
                        out_ref[r, sub * per:(sub + 1) * per, hd * LANE:(hd + 1) * LANE] = picked.astype(BF16)

    rc, rs1, rs2 = rc_ref[rows, :], rs1_ref[rows, :], rs2_ref[rows, :]
    cq = proj(COL_CQ, COL_CKV)
    ckv = proj(COL_CKV, COL_KR)
    k_rope = proj(COL_KR, COL_MQ)
    cq = (cq * _rinv(cq, MLA_Q_LORA) * gqa_ref[...]).astype(BF16)
    ckv = (ckv * _rinv(ckv, MLA_KV_LORA) * gkva_ref[...]).astype(BF16)
    q = jnp.dot(cq, wqb_ref[...], preferred_element_type=F32)
    kv = jnp.dot(ckv, wkvb_ref[...], preferred_element_type=F32)
    ones_lane = (lax.broadcasted_iota(jnp.int32, (1, LANE), 1) == MLA_V).astype(F32)

    def latent_heads(lo, hi):
        for hd in range(lo, hi):
            sl = slice(hd * LANE, (hd + 1) * LANE)
            qh = q[:, sl]
            qh = qh * _rinv(qh, MLA_QK) * gmq_ref[...]
            qm_ref[rows, sl] = (_rope(qh, rc, rs1, rs2) * MLA_QSCALE).astype(BF16)
            kh = kv[:, sl] + k_rope
            kh = kh * _rinv(kh, MLA_QK) * gmk_ref[...]
            km_ref[rows, sl] = _rope(kh, rc, rs1, rs2).astype(BF16)
            vm_ref[rows, sl] = (kv[:, MLA_PAD + hd * LANE:MLA_PAD + (hd + 1) * LANE] + ones_lane).astype(BF16)

    third = MLA_HEADS // 3
    emit_dilated(proj(COL_DQ, COL_DK), 0, gdq_ref, DIL_QSCALE)
    latent_heads(0, third)
    emit_dilated(proj(COL_DK, COL_DV), 1, gdk_ref, 1.0)
    latent_heads(third, 2 * third)
    zmq = proj(COL_MQ, COL_END)
    for hd in range(MEM_HEADS):
        sl = slice(hd * LANE, (hd + 1) * LANE)
        mq_ref[rows, sl] = head_norm(zmq[:, sl], gmemq_ref[:, sl], MEM_QSCALE).astype(BF16)
    latent_heads(2 * third, MLA_HEADS)
    emit_dilated(proj(COL_DV, COL_CQ), 2, None, 1.0)


def _in_proj(xf, p, layer, batch, seq):
    tokens = xf.shape[0]
    tm = IN_TILE
    res = functools.partial(_resident, layer)
    pos_blocks = seq // tm
    row = lambda w: pl.BlockSpec((tm, w), lambda i: (i, 0))
    rope_spec = pl.BlockSpec((tm, LANE), lambda i: (i % pos_blocks, 0))
    out_specs, out_shape, scratch = [], [], []
    for _ in range(3):
        for _, dilation in DIL_GROUPS:
            if dilation == 1:
                out_specs.append(row(DIL_OUT))
                out_shape.append(jax.ShapeDtypeStruct((tokens, DIL_OUT), BF16))
            else:
                assert IN_SUB % (dilation * 16) == 0
                out_specs.append(pl.BlockSpec((None, dilation, tm // dilation, DIL_OUT),
                                              lambda i: (i // pos_blocks, 0, i % pos_blocks, 0)))
                out_shape.append(jax.ShapeDtypeStruct((batch, dilation, seq // dilation, DIL_OUT), BF16))
                scratch.append(pltpu.VMEM((DIL_HEADS, IN_SUB, LANE), F32))
    for w in (MLA_PAD, MLA_PAD, MLA_PAD, MEM_WIDTH):
        out_specs.append(row(w))
        out_shape.append(jax.ShapeDtypeStruct((tokens, w), BF16))
    return pl.pallas_call(
        _in_kernel,
        grid=(tokens // tm,),
        in_specs=[row(D_MODEL), res((1, D_MODEL)), res((D_MODEL, COL_END)),
                  res((1, DIL_WIDTH)), res((1, DIL_WIDTH)),
                  res((1, MLA_Q_LORA)), res((1, MLA_KV_LORA)),
                  res((MLA_Q_LORA, MLA_PAD)), res((MLA_KV_LORA, 2 * MLA_PAD)),
                  res((1, LANE)), res((1, LANE)), res((1, MEM_WIDTH)),
                  rope_spec, rope_spec, rope_spec],
        out_specs=out_specs,
        out_shape=out_shape,
        scratch_shapes=scratch,
        compiler_params=_params(("parallel",)),
        name="in_proj",
    )(xf, p["g_mix"], p["w_in"], p["g_dq"], p["g_dk"], p["g_qa"], p["g_kva"], p["w_qb"], p["w_kvb"],
      p["g_mq"], p["g_mk"], p["g_memq"], p["rope_c"], p["rope_s1"], p["rope_s2"])


def _dil_group(qkv_refs, state, o_ref, *, first, slopes, dilation, sub_len, tl, sub):
    q_ref, kp_ref, kc_ref, kn_ref, vp_ref, vc_ref, vn_ref = qkv_refs
    acc_st, m_st, d_st = state
    last = o_ref is not None

    def window(prev_ref, cur_ref, next_ref, r, r0, sl):
        lo, hi = r0 - DIL_R, r0 + sub + DIL_R
        parts = [prev_ref[r, :, sl]] if lo < 0 else []
        parts.append(cur_ref[r, max(lo, 0):min(hi, tl), sl])
        if hi > tl:
            parts.append(next_ref[r, :, sl])
        return parts[0] if len(parts) == 1 else jnp.concatenate(parts, axis=0)

    nkeys = sub + 2 * DIL_R
    qi = lax.broadcasted_iota(jnp.int32, (sub, nkeys), 0)
    kj = lax.broadcasted_iota(jnp.int32, (sub, nkeys), 1)
    absrel = jnp.abs(kj - DIL_R - qi)
    band = absrel <= DIL_R
    dist = absrel.astype(F32)
    block_start = pl.program_id(1) * tl
    n_sub = tl // sub
    bias_cache = {}

    def masked_bias(sb, hd):
        edge = sb == 0 or sb == n_sub - 1
        key = (sb if edge else -1, hd)
        if key not in bias_cache:
            ok = band
            if edge:
                upos = kj + (block_start + (sb * sub - DIL_R))
                ok = band & (upos >= 0) & (upos < sub_len)
            bias_cache[key] = jnp.where(ok, dist * (-float(slopes[hd]) * dilation * LOG2E), NEG_INF)
        return bias_cache[key]

    units = [(r, sb) for r in range(dilation) for sb in range(tl // sub)]
    batch = max(1, DIL_BATCH_ROWS // sub)
    cols = lambda hd: slice(hd * LANE, (hd + 1) * LANE)

    def score_stage(blocks):
        out = []
        for r, sb, hd in blocks:
            r0 = sb * sub
            s = lax.dot_general(q_ref[r, r0:r0 + sub, cols(hd)], window(kp_ref, kc_ref, kn_ref, r, r0, cols(hd)),
                                NT_DIMS, preferred_element_type=F32)
            out.append(s + masked_bias(sb, hd))
        return out

    def softmax_stage(blocks, scores):
        out = []
        for s in scores:
            m = jnp.max(s, axis=-1, keepdims=True)
            pr = jnp.exp2(s - m)
            out.append((m, pr.astype(BF16)))
        return out

    def value_stage(blocks, stats):
        out = []
        ones = jnp.ones((nkeys, LANE), BF16)
        for (r, sb, hd), (m, pr) in zip(blocks, stats):
            r0 = sb * sub
            v_ext = jnp.concatenate([window(vp_ref, vc_ref, vn_ref, r, r0, cols(hd)), ones], axis=-1)
            res = jnp.dot(pr, v_ext, preferred_element_type=F32)
            out.append((res[:, :LANE], m, res[:, LANE:]))
        return out

    def merge_stage(blocks, outs):
        for (r, sb, hd), (acc, m, den) in zip(blocks, outs):
            r0 = sb * sub
            if dilation == 1:
                rows = slice(r0, r0 + sub)
            else:
                rows = pl.ds(r0 * dilation + r, sub, stride=dilation)
            if not first:
                m_prev = m_st[hd, rows, :]
                m_new = jnp.maximum(m_prev, m)
                w_prev = jnp.exp2(m_prev - m_new)
                w_cur = jnp.exp2(m - m_new)
                acc = w_prev * acc_st[hd, rows, :] + w_cur * acc
                den = w_prev * d_st[hd, rows, :] + w_cur * den
                m = m_new
            if last:
                o_ref[hd, rows, :] = acc / den
            else:
                acc_st[hd, rows, :] = acc
                m_st[hd, rows, :] = jnp.broadcast_to(m, (sub, LANE))
                d_st[hd, rows, :] = jnp.broadcast_to(den, (sub, LANE))

    stages = (score_stage, softmax_stage, value_stage, merge_stage)
    return [(stages, [(r, sb, hd) for r, sb in units[u0:u0 + batch] for hd in range(DIL_HEADS)])
            for u0 in range(0, len(units), batch)]


def _dil_geometry(g, seq):
    window, dilation = DIL_GROUPS[g]
    assert window // (2 * dilation) == DIL_R
    sub_len = seq // dilation
    assert sub_len * dilation == seq and sub_len % DIL_R == 0
    tl = min(sub_len, DIL_TOKENS // dilation)
    sub = min(tl, DIL_SUB)
    assert sub_len % tl == 0 and tl % sub == 0 and tl % DIL_R == 0
    return dict(slopes=_alibi_slopes()[g], dilation=dilation, sub_len=sub_len, tl=tl, sub=sub)


def _dil_kernel(*refs, geometry):
    n_in = 7 * N_DIL
    o_ref = refs[n_in]
    state = refs[n_in + 1:]
    work = []
    for g, geo in enumerate(geometry):
        work += _dil_group(refs[7 * g:7 * g + 7], state, o_ref if g == N_DIL - 1 else None, first=g == 0, **geo)
    n_stage = 4
    carried = [None] * len(work)
    for t in range(len(work) + n_stage - 1):
        for s in reversed(range(n_stage)):
            i = t - s
            if 0 <= i < len(work):
                stages, blocks = work[i]
                carried[i] = stages[s](blocks) if s == 0 else stages[s](blocks, carried[i])


def _dilated(dil_qkv, batch, seq):
    block = min(seq, DIL_TOKENS)
    geometry = [_dil_geometry(g, seq) for g in range(N_DIL)]
    args, in_specs = [], []
    for g, geo in enumerate(geometry):
        dilation, tl, sub_len = geo["dilation"], geo["tl"], geo["sub_len"]
        assert tl * dilation == block
        edge_blocks = sub_len // DIL_R
        per_tl = tl // DIL_R
        cur_spec = pl.BlockSpec((None, dilation, tl, DIL_OUT), lambda b, i: (b, 0, i, 0))
        prev_spec = pl.BlockSpec((None, dilation, DIL_R, DIL_OUT),
                                 lambda b, i, per_tl=per_tl: (b, 0, jnp.maximum(i * per_tl - 1, 0), 0))
        next_spec = pl.BlockSpec(
            (None, dilation, DIL_R, DIL_OUT),
            lambda b, i, per_tl=per_tl, edge_blocks=edge_blocks:
                (b, 0, jnp.minimum((i + 1) * per_tl, edge_blocks - 1), 0))
        q, k, v = (dil_qkv[t * N_DIL + g].reshape(batch, dilation, sub_len, DIL_OUT) for t in range(3))
        args += [q, k, k, k, v, v, v]
        in_specs += [cur_spec, prev_spec, cur_spec, next_spec, prev_spec, cur_spec, next_spec]
    scratch = [pltpu.VMEM((DIL_HEADS, block, LANE), F32)] * 3
    return pl.pallas_call(
        functools.partial(_dil_kernel, geometry=geometry),
        grid=(batch, seq // block),
        in_specs=in_specs,
        out_specs=pl.BlockSpec((None, DIL_HEADS, block, LANE), lambda b, i: (b, 0, i, 0)),
        out_shape=jax.ShapeDtypeStruct((batch, DIL_HEADS, seq, LANE), F32),
        scratch_shapes=scratch,
        compiler_params=_params(("parallel", "parallel")),
        name="dilated_attn",
    )(*args)


def _mla_kernel(q_ref, k_ref, v_ref, o_ref, s_buf, p_buf, *, tq, tk, nk):
    assert nk >= 2 and nk % 2 == 0

    def sub_tile(sub, carry):
        _mla_rows(pl.ds(pl.multiple_of(sub * tq, tq), tq), q_ref, k_ref, v_ref, o_ref, s_buf, p_buf,
                  tq=tq, tk=tk, nk=nk)
        return carry

    lax.fori_loop(0, q_ref.shape[0] // tq, sub_tile, 0)


def _mla_rows(rows, q_ref, k_ref, v_ref, o_ref, s_buf, p_buf, *, tq, tk, nk):
    for grp in range(MLA_HEADS // MLA_GROUP):
        cols = [slice(hd * LANE, (hd + 1) * LANE) for hd in range(grp * MLA_GROUP, (grp + 1) * MLA_GROUP)]
        qs = [q_ref[rows, sl] for sl in cols]

        def tile(j):
            return slice(j * tk, (j + 1) * tk)

        def scores(j, h, slot):
            s = lax.dot_general(k_ref[tile(j), cols[h]], qs[h], NT_DIMS, preferred_element_type=F32)
            s_buf[slot, h] = s
            return jnp.max(s, axis=0, keepdims=True)

        def numerators(h, slot, m, tile_max):
            m_new = jnp.maximum(m, tile_max)
            p_buf[slot, h] = jnp.exp2(s_buf[slot, h] - m_new).astype(BF16)
            return m_new, jnp.exp2(m - m_new)

        def accumulate(j, h, slot, alpha, acc):
            pv = lax.dot_general(v_ref[tile(j), cols[h]], p_buf[slot, h], TN_DIMS,
                                 preferred_element_type=F32)
            return alpha * acc + pv

        def trip(j, slot, state):
            out = []
            for h, (m, alpha, tile_max, acc) in enumerate(state):
                acc = accumulate(j - 1, h, 1 - slot, alpha, acc)
                m, alpha = numerators(h, slot, m, tile_max)
                out.append((m, alpha, scores(j + 1, h, 1 - slot), acc))
            return tuple(out)

        state = []
        for h in range(MLA_GROUP):
            m0 = jnp.full((1, tq), -jnp.inf, F32)
            m, alpha = numerators(h, 0, m0, scores(0, h, 0))
            state.append((m, alpha, scores(1, h, 1), jnp.zeros((LANE, tq), F32)))

        for j in range(1, nk - 1):
            state = trip(j, j % 2, state)
        last = (nk - 1) % 2
        tops = []
        for h, (m, alpha, tile_max, acc) in enumerate(state):
            acc = accumulate(nk - 2, h, 1 - last, alpha, acc)
            m, alpha = numerators(h, last, m, tile_max)
            acc = accumulate(nk - 1, h, last, alpha, acc)
            tops.append((acc / acc[MLA_V:MLA_V + 1, :])[:MLA_V])
        for i in range(MLA_GROUP // 2):
            col = (grp * MLA_GROUP // 2 + i) * LANE
            o_ref[rows, col:col + LANE] = jnp.concatenate(tops[2 * i:2 * i + 2], axis=0).T.astype(BF16)


def _mla(qm, km, vm, batch, seq):
    tq, tk = MLA_TQ, MLA_TK
    view = lambda t: t.reshape(batch, seq, MLA_PAD)
    kv_spec = pl.BlockSpec((None, seq, MLA_PAD), lambda b, i: (b, 0, 0))
    out = pl.pallas_call(
        functools.partial(_mla_kernel, tq=tq, tk=tk, nk=seq // tk),
        grid=(batch, seq // MLA_QBLOCK),
        in_specs=[pl.BlockSpec((None, MLA_QBLOCK, MLA_PAD), lambda b, i: (b, i, 0)), kv_spec, kv_spec],
        out_specs=pl.BlockSpec((None, MLA_QBLOCK, MLA_HEADS * MLA_V), lambda b, i: (b, i, 0)),
        out_shape=jax.ShapeDtypeStruct((batch, seq, MLA_HEADS * MLA_V), BF16),
        scratch_shapes=[pltpu.VMEM((2, MLA_GROUP, tk, tq), F32), pltpu.VMEM((2, MLA_GROUP, tk, tq), BF16)],
        compiler_params=_params(("parallel", "arbitrary")),
        name="latent_attn",
    )(view(qm), view(km), view(vm))
    return out.reshape(batch * seq, MLA_HEADS * MLA_V)


def _memkv_kernel(mem_ref, g_ref, w_ref, gk_ref, k_ref, v_ref):
    x = mem_ref[...]
    h = (x * _rinv(x, D_MODEL) * g_ref[...]).astype(BF16)
    kv = jnp.dot(h, w_ref[...], preferred_element_type=F32)
    for hd in range(MEM_HEADS):
        sl = slice(hd * LANE, (hd + 1) * LANE)
        blk = kv[:, sl]
        k_ref[:, sl] = (blk * _rinv(blk, MEM_HEAD_DIM) * gk_ref[...]).astype(BF16)
    v_ref[...] = kv[:, MEM_WIDTH:].astype(BF16)


def _mem_kv(mem, p, layer):
    batch, n_mem, _ = mem.shape
    res = functools.partial(_resident, layer)
    spec = lambda w: pl.BlockSpec((None, n_mem, w), lambda b: (b, 0, 0))
    return pl.pallas_call(
        _memkv_kernel,
        grid=(batch,),
        in_specs=[spec(D_MODEL), res((1, D_MODEL)), res((D_MODEL, 2 * MEM_WIDTH)), res((1, LANE))],
        out_specs=[spec(MEM_WIDTH), spec(MEM_WIDTH)],
        out_shape=[jax.ShapeDtypeStruct((batch, n_mem, MEM_WIDTH), BF16)] * 2,
        compiler_params=_params(("parallel",)),
        name="memory_kv",
    )(mem, p["g_mem"], p["w_memkv"], p["g_memk"])


def _memory_attention(q, k_ref, v_ref):
    outs = []
    for hd in range(MEM_HEADS):
        sl = slice(hd * LANE, (hd + 1) * LANE)
        s = lax.dot_general(q[:, sl], k_ref[:, sl], NT_DIMS, preferred_element_type=F32)
        m = jnp.max(s, axis=-1, keepdims=True)
        pr = jnp.exp2(s - m)
        den = jnp.sum(pr, axis=-1, keepdims=True)
        o = jnp.dot(pr.astype(BF16), v_ref[:, sl], preferred_element_type=F32) / den
        outs.append(o.astype(BF16))
    return jnp.concatenate(outs, axis=-1)


def _post_kernel(x_ref, a_ref, b_ref, mq_ref, kmem_ref, vmem_ref, gmix_ref, wg_ref, wb_ref, wo_ref, gffn_ref,
                 w1_ref, w2_ref, o_ref):
    x = x_ref[...]
    h = (x * _rinv(x, D_MODEL) * gmix_ref[...]).astype(BF16)
    a = jnp.concatenate([a_ref[hd] for hd in range(DIL_HEADS)], axis=-1).astype(BF16)
    mixed = None
    m = _memory_attention(mq_ref[...], kmem_ref, vmem_ref)
    for i, br in enumerate((a, b_ref[...], m)):
        proj = jnp.dot(br, wb_ref[i], preferred_element_type=F32)
        logits = jnp.dot(h, wg_ref[:, i * D_MODEL:(i + 1) * D_MODEL], preferred_element_type=F32)
        gate = jax.nn.sigmoid(logits)
        mixed = gate * proj if mixed is None else mixed + gate * proj
    x1 = x + jnp.dot(mixed.astype(BF16), wo_ref[...], preferred_element_type=F32)
    h2 = (x1 * _rinv(x1, D_MODEL) * gffn_ref[...]).astype(BF16)
    acc = x1
    for c in range(D_FF // FF_CHUNK):
        sl = slice(c * FF_CHUNK, (c + 1) * FF_CHUNK)
        u = jnp.dot(h2, w1_ref[:, sl], preferred_element_type=F32)
        u = jnp.square(jnp.maximum(u, 0.0)).astype(BF16)
        acc = acc + jnp.dot(u, w2_ref[sl, :], preferred_element_type=F32)
    o_ref[...] = acc


def _post(xf, a, b, mq, kmem, vmem, p, layer):
    tokens = xf.shape[0]
    tm = POST_TILE
    res = functools.partial(_resident, layer)
    row = lambda w: pl.BlockSpec((tm, w), lambda i: (i, 0))
    seq_blocks = a.shape[2] // tm
    a_spec = pl.BlockSpec((None, DIL_HEADS, tm, LANE), lambda i: (i // seq_blocks, 0, i % seq_blocks, 0))
    mem_spec = pl.BlockSpec((None, kmem.shape[1], MEM_WIDTH), lambda i: (i // seq_blocks, 0, 0))
    return pl.pallas_call(
        _post_kernel,
        grid=(tokens // tm,),
        in_specs=[row(D_MODEL), a_spec, row(BRANCH_WIDTH), row(MEM_WIDTH), mem_spec, mem_spec,
                  res((1, D_MODEL)), res((D_MODEL, GATE_WIDTH)),
                  res((N_BRANCH, BRANCH_WIDTH, D_MODEL)), res((D_MODEL, D_MODEL)),
                  res((1, D_MODEL)), res((D_MODEL, D_FF)), res((D_FF, D_MODEL))],
        out_specs=row(D_MODEL),
        out_shape=jax.ShapeDtypeStruct((tokens, D_MODEL), F32),
        compiler_params=_params(("parallel",)),
        name="merge_ffn",
    )(xf, a, b, mq, kmem, vmem, p["g_mix"], p["w_gate"], p["w_branch"], p["w_out"], p["g_ffn"], p["w_ff1"], p["w_ff2"])


def _rope_tables(seq):
    half = MLA_ROPE // 2
    inv = ROPE_THETA ** (-jnp.arange(half, dtype=F32) * 2.0 / MLA_ROPE)
    ang = jnp.arange(seq, dtype=F32)[:, None] * inv[None, :]
    c, s = jnp.cos(ang), jnp.sin(ang)
    z = lambda w: jnp.zeros((seq, w), F32)
    tail = LANE - MLA_QK
    rope_c = jnp.concatenate([jnp.ones((seq, MLA_NOPE), F32), c, c, z(tail)], axis=-1)
    rope_s1 = jnp.concatenate([z(MLA_NOPE), -s, z(half), z(tail)], axis=-1)
    rope_s2 = jnp.concatenate([z(MLA_NOPE), z(half), s, z(tail)], axis=-1)
    return rope_c, rope_s1, rope_s2


def _pad_lanes(t, width):
    return jnp.pad(t, [(0, 0)] * (t.ndim - 1) + [(0, width - t.shape[-1])])


def _prep_weights(mix_norm, w_in, dil_q_norm, dil_k_norm, mla_q_a_norm, mla_kv_a_norm, w_mla_q_b,
                  w_mla_kv_b, mla_q_norm, mla_k_norm, mem_norm, w_mem_kv, mem_q_norm, mem_k_norm,
                  w_branch, w_out, ffn_norm, w_ff1, w_ff2):
    depth = w_in.shape[0]
    o_cq = 3 * DIL_WIDTH
    o_ckv = o_cq + MLA_Q_LORA
    o_kr = o_ckv + MLA_KV_LORA
    o_mq = o_kr + MLA_ROPE
    o_gl = o_mq + MEM_WIDTH
    kr = jnp.pad(w_in[:, :, o_kr:o_mq], ((0, 0), (0, 0), (MLA_NOPE, LANE - MLA_QK)))
    w_r = jnp.concatenate([w_in[:, :, :o_kr], kr, w_in[:, :, o_mq:o_gl]], axis=2).astype(BF16)
    w_qb = _pad_lanes(w_mla_q_b.reshape(depth, MLA_Q_LORA, MLA_HEADS, MLA_QK), LANE)
    w_kvb = w_mla_kv_b.reshape(depth, MLA_KV_LORA, MLA_HEADS, MLA_NOPE + MLA_V)
    w_kb = _pad_lanes(w_kvb[..., :MLA_NOPE], LANE).reshape(depth, MLA_KV_LORA, MLA_PAD)
    w_vb = _pad_lanes(w_kvb[..., MLA_NOPE:], LANE).reshape(depth, MLA_KV_LORA, MLA_PAD)
    row = lambda g: g.reshape(depth, 1, -1)
    return {
        "g_mix": row(mix_norm),
        "w_in": w_r,
        "w_gate": w_in[:, :, o_gl:].astype(BF16),
        "g_dq": row(dil_q_norm),
        "g_dk": row(dil_k_norm),
        "g_qa": row(mla_q_a_norm),
        "g_kva": row(mla_kv_a_norm),
        "w_qb": w_qb.reshape(depth, MLA_Q_LORA, MLA_PAD).astype(BF16),
        "w_kvb": jnp.concatenate([w_kb, w_vb], axis=2).astype(BF16),
        "g_mq": _pad_lanes(row(mla_q_norm), LANE),
        "g_mk": _pad_lanes(row(mla_k_norm), LANE),
        "g_memq": jnp.tile(row(mem_q_norm), (1, 1, MEM_HEADS)),
        "g_mem": row(mem_norm),
        "w_memkv": w_mem_kv.astype(BF16),
        "g_memk": row(mem_k_norm),
        "w_branch": w_branch.astype(BF16),
        "w_out": w_out.astype(BF16),
        "g_ffn": row(ffn_norm),
        "w_ff1": w_ff1.astype(BF16),
        "w_ff2": w_ff2.astype(BF16),
    }


def _encoder_layer(x, mem, p, layer):
    batch, seq, _ = x.shape
    xf = x.reshape(batch * seq, D_MODEL)
    outs = _in_proj(xf, p, layer, batch, seq)
    dil_qkv, (qm, km, vm, mq) = outs[:3 * N_DIL], outs[3 * N_DIL:]
    a = _dilated(dil_qkv, batch, seq)
    b = _mla(qm, km, vm, batch, seq)
    kmem, vmem = _mem_kv(mem, p, layer)
    return _post(xf, a, b, mq, kmem, vmem, p, layer).reshape(batch, seq, D_MODEL)


def kernel(x_prompt, x_sample, mem_prompt, mem_sample, mix_norm, w_in, dil_q_norm, dil_k_norm, mla_q_a_norm, mla_kv_a_norm, w_mla_q_b, w_mla_kv_b, mla_q_norm, mla_k_norm, mem_norm, w_mem_kv, mem_q_norm, mem_k_norm, w_branch, w_out, ffn_norm, w_ff1, w_ff2):
    weights = (mix_norm, w_in, dil_q_norm, dil_k_norm, mla_q_a_norm, mla_kv_a_norm, w_mla_q_b,
               w_mla_kv_b, mla_q_norm, mla_k_norm, mem_norm, w_mem_kv, mem_q_norm, mem_k_norm,
               w_branch, w_out, ffn_norm, w_ff1, w_ff2)
    yp, ys = x_prompt, x_sample
    p = _prep_weights(*weights)
    names = ("rope_c", "rope_s1", "rope_s2")
    p_prompt = {**p, **dict(zip(names, _rope_tables(yp.shape[1])))}
    p_sample = {**p, **dict(zip(names, _rope_tables(ys.shape[1])))}
    for layer in range(mix_norm.shape[0]):
        yp = _encoder_layer(yp, mem_prompt, p_prompt, layer)
        ys = _encoder_layer(ys, mem_sample, p_sample, layer)
    return (yp, ys)
```

```python
import functools
import math

import numpy as np
import jax
import jax.numpy as jnp
from jax import lax
from jax.experimental import pallas as pl
from jax.experimental.pallas import tpu as pltpu

F32 = jnp.float32
BF16 = jnp.bfloat16

LANE = 128
VMEM_LIMIT_BYTES = 56 * 1024 * 1024

D_MODEL = 1024
EPS = 1e-6
NEG_INF = -1e30
LOG2E = math.log2(math.e)
DIL_GROUPS = ((128, 1), (512, 4), (2048, 16))
N_DIL = 3
DIL_HEADS = 4
DIL_HEAD_DIM = 128
DIL_WIDTH = N_DIL * DIL_HEADS * DIL_HEAD_DIM
DIL_OUT = DIL_HEADS * DIL_HEAD_DIM
MLA_HEADS = 8
MLA_Q_LORA = 384
MLA_KV_LORA = 256
MLA_NOPE = 64
MLA_ROPE = 32
MLA_QK = MLA_NOPE + MLA_ROPE
MLA_V = 64
MLA_PAD = MLA_HEADS * LANE
ROPE_THETA = 10000.0
MEM_HEADS = 4
MEM_HEAD_DIM = 128
MEM_WIDTH = MEM_HEADS * MEM_HEAD_DIM
N_BRANCH = 3
BRANCH_WIDTH = 512
D_FF = 4 * D_MODEL
GATE_WIDTH = N_BRANCH * D_MODEL

COL_DQ = 0
COL_DK = COL_DQ + DIL_WIDTH
COL_DV = COL_DK + DIL_WIDTH
COL_CQ = COL_DV + DIL_WIDTH
COL_CKV = COL_CQ + MLA_Q_LORA
COL_KR = COL_CKV + MLA_KV_LORA
COL_MQ = COL_KR + LANE
COL_END = COL_MQ + MEM_WIDTH

DIL_R = 64
DIL_SUB = 128
DIL_TOKENS = 1024
DIL_BATCH_ROWS = 128
DIL_QSCALE = LOG2E / math.sqrt(DIL_HEAD_DIM)
MLA_QSCALE = LOG2E / math.sqrt(MLA_QK)
MEM_QSCALE = LOG2E / math.sqrt(MEM_HEAD_DIM)

IN_TILE = 512
IN_SUB = 256
POST_TILE = 512
MLA_TQ = 256
MLA_QBLOCK = 512
MLA_TK = 512
MLA_GROUP = 8
FF_CHUNK = 1024

NT_DIMS = (((1,), (1,)), ((), ()))
TN_DIMS = (((0,), (0,)), ((), ()))


def _alibi_slopes():
    n = N_DIL * DIL_HEADS
    s = np.array([2.0 ** (-8.0 * (k + 1) / n) for k in range(n)], dtype=np.float32)
    return s.reshape(N_DIL, DIL_HEADS)


def _rinv(x, width):
    return lax.rsqrt(jnp.sum(x * x, axis=-1, keepdims=True) * (1.0 / width) + EPS)


def _resident(layer, shape):
    nd = len(shape)
    return pl.BlockSpec((None,) + tuple(shape), lambda *_: (layer,) + (0,) * nd, pipeline_mode=pl.Buffered(1))


def _params(semantics):
    return pltpu.CompilerParams(dimension_semantics=semantics, vmem_limit_bytes=VMEM_LIMIT_BYTES)


def _rope(t, c, s1, s2):
    return t * c + pltpu.roll(t, LANE - MLA_ROPE // 2, 1) * s1 + pltpu.roll(t, MLA_ROPE // 2, 1) * s2


def _in_kernel(x_ref, gmix_ref, w_ref, gdq_ref, gdk_ref, gqa_ref, gkva_ref, wqb_ref, wkvb_ref,
               gmq_ref, gmk_ref, gmemq_ref, rc_ref, rs1_ref, rs2_ref, *refs):
    for sub in range(x_ref.shape[0] // IN_SUB):
        _in_rows(sub, x_ref, gmix_ref, w_ref, gdq_ref, gdk_ref, gqa_ref, gkva_ref, wqb_ref, wkvb_ref,
                 gmq_ref, gmk_ref, gmemq_ref, rc_ref, rs1_ref, rs2_ref, *refs)


def _in_rows(sub, x_ref, gmix_ref, w_ref, gdq_ref, gdk_ref, gqa_ref, gkva_ref, wqb_ref, wkvb_ref,
             gmq_ref, gmk_ref, gmemq_ref, rc_ref, rs1_ref, rs2_ref, *refs):
    dil_refs, (qm_ref, km_ref, vm_ref, mq_ref) = refs[:3 * N_DIL], refs[3 * N_DIL:3 * N_DIL + 4]
    stage_refs = refs[3 * N_DIL + 4:]
    tm = IN_SUB
    rows = slice(sub * tm, (sub + 1) * tm)
    x = x_ref[rows, :]
    h = (x * _rinv(x, D_MODEL) * gmix_ref[...]).astype(BF16)

    def proj(lo, hi):
        return jnp.dot(h, w_ref[:, lo:hi], preferred_element_type=F32)

    def head_norm(blk, gain, scale):
        return blk * _rinv(blk, LANE) * gain * scale

    def emit_dilated(z, t, gains_ref, scale):
        for g, (_, dilation) in enumerate(DIL_GROUPS):
            out_ref = dil_refs[t * N_DIL + g]
            stage = None if dilation == 1 else stage_refs[t * (N_DIL - 1) + g - 1]
            for hd in range(DIL_HEADS):
                col = (g * DIL_HEADS + hd) * LANE
                blk = z[:, col:col + LANE]
                if gains_ref is not None:
                    blk = head_norm(blk, gains_ref[:, col:col + LANE], scale)
                if dilation == 1:
                    out_ref[rows, hd * LANE:(hd + 1) * LANE] = blk.astype(BF16)
                else:
                    stage[hd] = blk
            if dilation > 1:
                per = tm // dilation
                for r in range(dilation):
                    for hd in range(DIL_HEADS):
                        picked = stage[hd, pl.ds(r, per, stride=dilation), :]
                        out_ref[r, sub * per:(sub + 1) * per, hd * LANE:(hd + 1) * LANE] = picked.astype(BF16)

    rc, rs1, rs2 = rc_ref[rows, :], rs1_ref[rows, :], rs2_ref[rows, :]
    cq = proj(COL_CQ, COL_CKV)
    ckv = proj(COL_CKV, COL_KR)
    k_rope = proj(COL_KR, COL_MQ)
    cq = (cq * _rinv(cq, MLA_Q_LORA) * gqa_ref[...]).astype(BF16)
    ckv = (ckv * _rinv(ckv, MLA_KV_LORA) * gkva_ref[...]).astype(BF16)
    q = jnp.dot(cq, wqb_ref[...], preferred_element_type=F32)
    kv = jnp.dot(ckv, wkvb_ref[...], preferred_element_type=F32)
    ones_lane = (lax.broadcasted_iota(jnp.int32, (1, LANE), 1) == MLA_V).astype(F32)

    def latent_heads(lo, hi):
        for hd in range(lo, hi):
            sl = slice(hd * LANE, (hd + 1) * LANE)
            qh = q[:, sl]
            qh = qh * _rinv(qh, MLA_QK) * gmq_ref[...]
            qm_ref[rows, sl] = (_rope(qh, rc, rs1, rs2) * MLA_QSCALE).astype(BF16)
            kh = kv[:, sl] + k_rope
            kh = kh * _rinv(kh, MLA_QK) * gmk_ref[...]
            km_ref[rows, sl] = _rope(kh, rc, rs1, rs2).astype(BF16)
            vm_ref[rows, sl] = (kv[:, MLA_PAD + hd * LANE:MLA_PAD + (hd + 1) * LANE] + ones_lane).astype(BF16)

    third = MLA_HEADS // 3
    emit_dilated(proj(COL_DQ, COL_DK), 0, gdq_ref, DIL_QSCALE)
    latent_heads(0, third)
    emit_dilated(proj(COL_DK, COL_DV), 1, gdk_ref, 1.0)
    latent_heads(third, 2 * third)
    zmq = proj(COL_MQ, COL_END)
    for hd in range(MEM_HEADS):
        sl = slice(hd * LANE, (hd + 1) * LANE)
        mq_ref[rows, sl] = head_norm(zmq[:, sl], gmemq_ref[:, sl], MEM_QSCALE).astype(BF16)
    latent_heads(2 * third, MLA_HEADS)
    emit_dilated(proj(COL_DV, COL_CQ), 2, None, 1.0)


def _in_proj(xf, p, layer, batch, seq):
    tokens = xf.shape[0]
    tm = IN_TILE
    res = functools.partial(_resident, layer)
    pos_blocks = seq // tm
    row = lambda w: pl.BlockSpec((tm, w), lambda i: (i, 0))
    rope_spec = pl.BlockSpec((tm, LANE), lambda i: (i % pos_blocks, 0))
    out_specs, out_shape, scratch = [], [], []
    for _ in range(3):
        for _, dilation in DIL_GROUPS:
            if dilation == 1:
                out_specs.append(row(DIL_OUT))
                out_shape.append(jax.ShapeDtypeStruct((tokens, DIL_OUT), BF16))
            else:
                assert IN_SUB % (dilation * 16) == 0
                out_specs.append(pl.BlockSpec((None, dilation, tm // dilation, DIL_OUT),
                                              lambda i: (i // pos_blocks, 0, i % pos_blocks, 0)))
                out_shape.append(jax.ShapeDtypeStruct((batch, dilation, seq // dilation, DIL_OUT), BF16))
                scratch.append(pltpu.VMEM((DIL_HEADS, IN_SUB, LANE), F32))
    for w in (MLA_PAD, MLA_PAD, MLA_PAD, MEM_WIDTH):
        out_specs.append(row(w))
        out_shape.append(jax.ShapeDtypeStruct((tokens, w), BF16))
    return pl.pallas_call(
        _in_kernel,
        grid=(tokens // tm,),
        in_specs=[row(D_MODEL), res((1, D_MODEL)), res((D_MODEL, COL_END)),
                  res((1, DIL_WIDTH)), res((1, DIL_WIDTH)),
                  res((1, MLA_Q_LORA)), res((1, MLA_KV_LORA)),
                  res((MLA_Q_LORA, MLA_PAD)), res((MLA_KV_LORA, 2 * MLA_PAD)),
                  res((1, LANE)), res((1, LANE)), res((1, MEM_WIDTH)),
                  rope_spec, rope_spec, rope_spec],
        out_specs=out_specs,
        out_shape=out_shape,
        scratch_shapes=scratch,
        compiler_params=_params(("parallel",)),
        name="in_proj",
    )(xf, p["g_mix"], p["w_in"], p["g_dq"], p["g_dk"], p["g_qa"], p["g_kva"], p["w_qb"], p["w_kvb"],
      p["g_mq"], p["g_mk"], p["g_memq"], p["rope_c"], p["rope_s1"], p["rope_s2"])


def _dil_group(qkv_refs, state, o_ref, *, first, slopes, dilation, sub_len, tl, sub):
    q_ref, kp_ref, kc_ref, kn_ref, vp_ref, vc_ref, vn_ref = qkv_refs
    acc_st, m_st, d_st = state
    last = o_ref is not None

    def window(prev_ref, cur_ref, next_ref, r, r0, sl):
        lo, hi = r0 - DIL_R, r0 + sub + DIL_R
        parts = [prev_ref[r, :, sl]] if lo < 0 else []
        parts.append(cur_ref[r, max(lo, 0):min(hi, tl), sl])
        if hi > tl:
            parts.append(next_ref[r, :, sl])
        return parts[0] if len(parts) == 1 else jnp.concatenate(parts, axis=0)

    nkeys = sub + 2 * DIL_R
    qi = lax.broadcasted_iota(jnp.int32, (sub, nkeys), 0)
    kj = lax.broadcasted_iota(jnp.int32, (sub, nkeys), 1)
    absrel = jnp.abs(kj - DIL_R - qi)
    band = absrel <= DIL_R
    dist = absrel.astype(F32)
    block_start = pl.program_id(1) * tl
    n_sub = tl // sub
    bias_cache = {}

    def masked_bias(sb, hd):
        edge = sb == 0 or sb == n_sub - 1
        key = (sb if edge else -1, hd)
        if key not in bias_cache:
            ok = band
            if edge:
                upos = kj + (block_start + (sb * sub - DIL_R))
                ok = band & (upos >= 0) & (upos < sub_len)
            bias_cache[key] = jnp.where(ok, dist * (-float(slopes[hd]) * dilation * LOG2E), NEG_INF)
        return bias_cache[key]

    units = [(r, sb) for r in range(dilation) for sb in range(tl // sub)]
    batch = max(1, DIL_BATCH_ROWS // sub)
    cols = lambda hd: slice(hd * LANE, (hd + 1) * LANE)

    def score_stage(blocks):
        out = []
        for r, sb, hd in blocks:
            r0 = sb * sub
            s = lax.dot_general(q_ref[r, r0:r0 + sub, cols(hd)], window(kp_ref, kc_ref, kn_ref, r, r0, cols(hd)),
                                NT_DIMS, preferred_element_type=F32)
            out.append(s + masked_bias(sb, hd))
        return out

    def softmax_stage(blocks, scores):
        out = []
        for s in scores:
            m = jnp.max(s, axis=-1, keepdims=True)
            pr = jnp.exp2(s - m)
            out.append((m, pr.astype(BF16)))
        return out

    def value_stage(blocks, stats):
        out = []
        ones = jnp.ones((nkeys, LANE), BF16)
        for (r, sb, hd), (m, pr) in zip(blocks, stats):
            r0 = sb * sub
            v_ext = jnp.concatenate([window(vp_ref, vc_ref, vn_ref, r, r0, cols(hd)), ones], axis=-1)
            res = jnp.dot(pr, v_ext, preferred_element_type=F32)
            out.append((res[:, :LANE], m, res[:, LANE:]))
        return out

    def merge_stage(blocks, outs):
        for (r, sb, hd), (acc, m, den) in zip(blocks, outs):
            r0 = sb * sub
            if dilation == 1:
                rows = slice(r0, r0 + sub)
            else:
                rows = pl.ds(r0 * dilation + r, sub, stride=dilation)
            if not first:
                m_prev = m_st[hd, rows, :]
                m_new = jnp.maximum(m_prev, m)
                w_prev = jnp.exp2(m_prev - m_new)
                w_cur = jnp.exp2(m - m_new)
                acc = w_prev * acc_st[hd, rows, :] + w_cur * acc
                den = w_prev * d_st[hd, rows, :] + w_cur * den
                m = m_new
            if last:
                o_ref[hd, rows, :] = acc / den
            else:
                acc_st[hd, rows, :] = acc
                m_st[hd, rows, :] = jnp.broadcast_to(m, (sub, LANE))
                d_st[hd, rows, :] = jnp.broadcast_to(den, (sub, LANE))

    stages = (score_stage, softmax_stage, value_stage, merge_stage)
    return [(stages, [(r, sb, hd) for r, sb in units[u0:u0 + batch] for hd in range(DIL_HEADS)])
            for u0 in range(0, len(units), batch)]


def _dil_geometry(g, seq):
    window, dilation = DIL_GROUPS[g]
    assert window // (2 * dilation) == DIL_R
    sub_len = seq // dilation
    assert sub_len * dilation == seq and sub_len % DIL_R == 0
    tl = min(sub_len, DIL_TOKENS // dilation)
    sub = min(tl, DIL_SUB)
    assert sub_len % tl == 0 and tl % sub == 0 and tl % DIL_R == 0
    return dict(slopes=_alibi_slopes()[g], dilation=dilation, sub_len=sub_len, tl=tl, sub=sub)


def _dil_kernel(*refs, geometry):
    n_in = 7 * N_DIL
    o_ref = refs[n_in]
    state = refs[n_in + 1:]
    work = []
    for g, geo in enumerate(geometry):
        work += _dil_group(refs[7 * g:7 * g + 7], state, o_ref if g == N_DIL - 1 else None, first=g == 0, **geo)
    n_stage = 4
    carried = [None] * len(work)
    for t in range(len(work) + n_stage - 1):
        for s in reversed(range(n_stage)):
            i = t - s
            if 0 <= i < len(work):
                stages, blocks = work[i]
                carried[i] = stages[s](blocks) if s == 0 else stages[s](blocks, carried[i])


def _dilated(dil_qkv, batch, seq):
    block = min(seq, DIL_TOKENS)
    geometry = [_dil_geometry(g, seq) for g in range(N_DIL)]
    args, in_specs = [], []
    for g, geo in enumerate(geometry):
        dilation, tl, sub_len = geo["dilation"], geo["tl"], geo["sub_len"]
        assert tl * dilation == block
        edge_blocks = sub_len // DIL_R
        per_tl = tl // DIL_R
        cur_spec = pl.BlockSpec((None, dilation, tl, DIL_OUT), lambda b, i: (b, 0, i, 0))
        prev_spec = pl.BlockSpec((None, dilation, DIL_R, DIL_OUT),
                                 lambda b, i, per_tl=per_tl: (b, 0, jnp.maximum(i * per_tl - 1, 0), 0))
        next_spec = pl.BlockSpec(
            (None, dilation, DIL_R, DIL_OUT),
            lambda b, i, per_tl=per_tl, edge_blocks=edge_blocks:
                (b, 0, jnp.minimum((i + 1) * per_tl, edge_blocks - 1), 0))
        q, k, v = (dil_qkv[t * N_DIL + g].reshape(batch, dilation, sub_len, DIL_OUT) for t in range(3))
        args += [q, k, k, k, v, v, v]
        in_specs += [cur_spec, prev_spec, cur_spec, next_spec, prev_spec, cur_spec, next_spec]
    scratch = [pltpu.VMEM((DIL_HEADS, block, LANE), F32)] * 3
    return pl.pallas_call(
        functools.partial(_dil_kernel, geometry=geometry),
        grid=(batch, seq // block),
        in_specs=in_specs,
        out_specs=pl.BlockSpec((None, DIL_HEADS, block, LANE), lambda b, i: (b, 0, i, 0)),
        out_shape=jax.ShapeDtypeStruct((batch, DIL_HEADS, seq, LANE), F32),
        scratch_shapes=scratch,
        compiler_params=_params(("parallel", "parallel")),
        name="dilated_attn",
    )(*args)


def _mla_kernel(q_ref, k_ref, v_ref, o_ref, s_buf, p_buf, *, tq, tk, nk):
    assert nk >= 2 and nk % 2 == 0

    def sub_tile(sub, carry):
        _mla_rows(pl.ds(pl.multiple_of(sub * tq, tq), tq), q_ref, k_ref, v_ref, o_ref, s_buf, p_buf,
                  tq=tq, tk=tk, nk=nk)
        return carry

    lax.fori_loop(0, q_ref.shape[0] // tq, sub_tile, 0)


def _mla_rows(rows, q_ref, k_ref, v_ref, o_ref, s_buf, p_buf, *, tq, tk, nk):
    for grp in range(MLA_HEADS // MLA_GROUP):
        cols = [slice(hd * LANE, (hd + 1) * LANE) for hd in range(grp * MLA_GROUP, (grp + 1) * MLA_GROUP)]
        qs = [q_ref[rows, sl] for sl in cols]

        def tile(j):
            return slice(j * tk, (j + 1) * tk)

        def scores(j, h, slot):
            s = lax.dot_general(k_ref[tile(j), cols[h]], qs[h], NT_DIMS, preferred_element_type=F32)
            s_buf[slot, h] = s
            return jnp.max(s, axis=0, keepdims=True)

        def numerators(h, slot, m, tile_max):
            m_new = jnp.maximum(m, tile_max)
            p_buf[slot, h] = jnp.exp2(s_buf[slot, h] - m_new).astype(BF16)
            return m_new, jnp.exp2(m - m_new)

        def accumulate(j, h, slot, alpha, acc):
            pv = lax.dot_general(v_ref[tile(j), cols[h]], p_buf[slot, h], TN_DIMS,
                                 preferred_element_type=F32)
            return alpha * acc + pv

        def trip(j, slot, state):
            out = []
            for h, (m, alpha, tile_max, acc) in enumerate(state):
                next_max = scores(j + 1, h, 1 - slot)
                acc = accumulate(j - 1, h, 1 - slot, alpha, acc)
                m, alpha = numerators(h, slot, m, tile_max)
                out.append((m, alpha, next_max, acc))
            return tuple(out)

        state = []
        for h in range(MLA_GROUP):
            m0 = jnp.full((1, tq), -jnp.inf, F32)
            m, alpha = numerators(h, 0, m0, scores(0, h, 0))
            state.append((m, alpha, scores(1, h, 1), jnp.zeros((LANE, tq), F32)))

        for j in range(1, nk - 1):
            state = trip(j, j % 2, state)
        last = (nk - 1) % 2
        tops = []
        for h, (m, alpha, tile_max, acc) in enumerate(state):
            acc = accumulate(nk - 2, h, 1 - last, alpha, acc)
            m, alpha = numerators(h, last, m, tile_max)
            acc = accumulate(nk - 1, h, last, alpha, acc)
            tops.append((acc / acc[MLA_V:MLA_V + 1, :])[:MLA_V])
        for i in range(MLA_GROUP // 2):
            col = (grp * MLA_GROUP // 2 + i) * LANE
            o_ref[rows, col:col + LANE] = jnp.concatenate(tops[2 * i:2 * i + 2], axis=0).T.astype(BF16)


def _mla(qm, km, vm, batch, seq):
    tq, tk = MLA_TQ, MLA_TK
    view = lambda t: t.reshape(batch, seq, MLA_PAD)
    kv_spec = pl.BlockSpec((None, seq, MLA_PAD), lambda b, i: (b, 0, 0))
    out = pl.pallas_call(
        functools.partial(_mla_kernel, tq=tq, tk=tk, nk=seq // tk),
        grid=(batch, seq // MLA_QBLOCK),
        in_specs=[pl.BlockSpec((None, MLA_QBLOCK, MLA_PAD), lambda b, i: (b, i, 0)), kv_spec, kv_spec],
        out_specs=pl.BlockSpec((None, MLA_QBLOCK, MLA_HEADS * MLA_V), lambda b, i: (b, i, 0)),
        out_shape=jax.ShapeDtypeStruct((batch, seq, MLA_HEADS * MLA_V), BF16),
        scratch_shapes=[pltpu.VMEM((2, MLA_GROUP, tk, tq), F32), pltpu.VMEM((2, MLA_GROUP, tk, tq), BF16)],
        compiler_params=_params(("parallel", "arbitrary")),
        name="latent_attn",
    )(view(qm), view(km), view(vm))
    return out.reshape(batch * seq, MLA_HEADS * MLA_V)


def _memkv_kernel(mem_ref, g_ref, w_ref, gk_ref, k_ref, v_ref):
    x = mem_ref[...]
    h = (x * _rinv(x, D_MODEL) * g_ref[...]).astype(BF16)
    kv = jnp.dot(h, w_ref[...], preferred_element_type=F32)
    for hd in range(MEM_HEADS):
        sl = slice(hd * LANE, (hd + 1) * LANE)
        blk = kv[:, sl]
        k_ref[:, sl] = (blk * _rinv(blk, MEM_HEAD_DIM) * gk_ref[...]).astype(BF16)
    v_ref[...] = kv[:, MEM_WIDTH:].astype(BF16)


def _mem_kv(mem, p, layer):
    batch, n_mem, _ = mem.shape
    res = functools.partial(_resident, layer)
    spec = lambda w: pl.BlockSpec((None, n_mem, w), lambda b: (b, 0, 0))
    return pl.pallas_call(
        _memkv_kernel,
        grid=(batch,),
        in_specs=[spec(D_MODEL), res((1, D_MODEL)), res((D_MODEL, 2 * MEM_WIDTH)), res((1, LANE))],
        out_specs=[spec(MEM_WIDTH), spec(MEM_WIDTH)],
        out_shape=[jax.ShapeDtypeStruct((batch, n_mem, MEM_WIDTH), BF16)] * 2,
        compiler_params=_params(("parallel",)),
        name="memory_kv",
    )(mem, p["g_mem"], p["w_memkv"], p["g_memk"])


def _memory_attention(q, k_ref, v_ref):
    outs = []
    for hd in range(MEM_HEADS):
        sl = slice(hd * LANE, (hd + 1) * LANE)
        s = lax.dot_general(q[:, sl], k_ref[:, sl], NT_DIMS, preferred_element_type=F32)
        m = jnp.max(s, axis=-1, keepdims=True)
        pr = jnp.exp2(s - m)
        den = jnp.sum(pr, axis=-1, keepdims=True)
        o = jnp.dot(pr.astype(BF16), v_ref[:, sl], preferred_element_type=F32) / den
        outs.append(o.astype(BF16))
    return jnp.concatenate(outs, axis=-1)


def _post_kernel(x_ref, a_ref, b_ref, mq_ref, kmem_ref, vmem_ref, gmix_ref, wg_ref, wb_ref, wo_ref, gffn_ref,
                 w1_ref, w2_ref, o_ref):
    x = x_ref[...]
    h = (x * _rinv(x, D_MODEL) * gmix_ref[...]).astype(BF16)
    a = jnp.concatenate([a_ref[hd] for hd in range(DIL_HEADS)], axis=-1).astype(BF16)
    mixed = None
    m = _memory_attention(mq_ref[...], kmem_ref, vmem_ref)
    for i, br in enumerate((a, b_ref[...], m)):
        proj = jnp.dot(br, wb_ref[i], preferred_element_type=F32)
        logits = jnp.dot(h, wg_ref[:, i * D_MODEL:(i + 1) * D_MODEL], preferred_element_type=F32)
        gate = jax.nn.sigmoid(logits)
        mixed = gate * proj if mixed is None else mixed + gate * proj
    x1 = x + jnp.dot(mixed.astype(BF16), wo_ref[...], preferred_element_type=F32)
    h2 = (x1 * _rinv(x1, D_MODEL) * gffn_ref[...]).astype(BF16)
    acc = x1
    for c in range(D_FF // FF_CHUNK):
        sl = slice(c * FF_CHUNK, (c + 1) * FF_CHUNK)
        u = jnp.dot(h2, w1_ref[:, sl], preferred_element_type=F32)
        u = jnp.square(jnp.maximum(u, 0.0)).astype(BF16)
        acc = acc + jnp.dot(u, w2_ref[sl, :], preferred_element_type=F32)
    o_ref[...] = acc


def _post(xf, a, b, mq, kmem, vmem, p, layer):
    tokens = xf.shape[0]
    tm = POST_TILE
    res = functools.partial(_resident, layer)
    row = lambda w: pl.BlockSpec((tm, w), lambda i: (i, 0))
    seq_blocks = a.shape[2] // tm
    a_spec = pl.BlockSpec((None, DIL_HEADS, tm, LANE), lambda i: (i // seq_blocks, 0, i % seq_blocks, 0))
    mem_spec = pl.BlockSpec((None, kmem.shape[1], MEM_WIDTH), lambda i: (i // seq_blocks, 0, 0))
    return pl.pallas_call(
        _post_kernel,
        grid=(tokens // tm,),
        in_specs=[row(D_MODEL), a_spec, row(BRANCH_WIDTH), row(MEM_WIDTH), mem_spec, mem_spec,
                  res((1, D_MODEL)), res((D_MODEL, GATE_WIDTH)),
                  res((N_BRANCH, BRANCH_WIDTH, D_MODEL)), res((D_MODEL, D_MODEL)),
                  res((1, D_MODEL)), res((D_MODEL, D_FF)), res((D_FF, D_MODEL))],
        out_specs=row(D_MODEL),
        out_shape=jax.ShapeDtypeStruct((tokens, D_MODEL), F32),
        compiler_params=_params(("parallel",)),
        name="merge_ffn",
    )(xf, a, b, mq, kmem, vmem, p["g_mix"], p["w_gate"], p["w_branch"], p["w_out"], p["g_ffn"], p["w_ff1"], p["w_ff2"])


def _rope_tables(seq):
    half = MLA_ROPE // 2
    inv = ROPE_THETA ** (-jnp.arange(half, dtype=F32) * 2.0 / MLA_ROPE)
    ang = jnp.arange(seq, dtype=F32)[:, None] * inv[None, :]
    c, s = jnp.cos(ang), jnp.sin(ang)
    z = lambda w: jnp.zeros((seq, w), F32)
    tail = LANE - MLA_QK
    rope_c = jnp.concatenate([jnp.ones((seq, MLA_NOPE), F32), c, c, z(tail)], axis=-1)
    rope_s1 = jnp.concatenate([z(MLA_NOPE), -s, z(half), z(tail)], axis=-1)
    rope_s2 = jnp.concatenate([z(MLA_NOPE), z(half), s, z(tail)], axis=-1)
    return rope_c, rope_s1, rope_s2


def _pad_lanes(t, width):
    return jnp.pad(t, [(0, 0)] * (t.ndim - 1) + [(0, width - t.shape[-1])])


def _prep_weights(mix_norm, w_in, dil_q_norm, dil_k_norm, mla_q_a_norm, mla_kv_a_norm, w_mla_q_b,
                  w_mla_kv_b, mla_q_norm, mla_k_norm, mem_norm, w_mem_kv, mem_q_norm, mem_k_norm,
                  w_branch, w_out, ffn_norm, w_ff1, w_ff2):
    depth = w_in.shape[0]
    o_cq = 3 * DIL_WIDTH
    o_ckv = o_cq + MLA_Q_LORA
    o_kr = o_ckv + MLA_KV_LORA
    o_mq = o_kr + MLA_ROPE
    o_gl = o_mq + MEM_WIDTH
    kr = jnp.pad(w_in[:, :, o_kr:o_mq], ((0, 0), (0, 0), (MLA_NOPE, LANE - MLA_QK)))
    w_r = jnp.concatenate([w_in[:, :, :o_kr], kr, w_in[:, :, o_mq:o_gl]], axis=2).astype(BF16)
    w_qb = _pad_lanes(w_mla_q_b.reshape(depth, MLA_Q_LORA, MLA_HEADS, MLA_QK), LANE)
    w_kvb = w_mla_kv_b.reshape(depth, MLA_KV_LORA, MLA_HEADS, MLA_NOPE + MLA_V)
    w_kb = _pad_lanes(w_kvb[..., :MLA_NOPE], LANE).reshape(depth, MLA_KV_LORA, MLA_PAD)
    w_vb = _pad_lanes(w_kvb[..., MLA_NOPE:], LANE).reshape(depth, MLA_KV_LORA, MLA_PAD)
    row = lambda g: g.reshape(depth, 1, -1)
    return {
        "g_mix": row(mix_norm),
        "w_in": w_r,
        "w_gate": w_in[:, :, o_gl:].astype(BF16),
        "g_dq": row(dil_q_norm),
        "g_dk": row(dil_k_norm),
        "g_qa": row(mla_q_a_norm),
        "g_kva": row(mla_kv_a_norm),
        "w_qb": w_qb.reshape(depth, MLA_Q_LORA, MLA_PAD).astype(BF16),
        "w_kvb": jnp.concatenate([w_kb, w_vb], axis=2).astype(BF16),
        "g_mq": _pad_lanes(row(mla_q_norm), LANE),
        "g_mk": _pad_lanes(row(mla_k_norm), LANE),
        "g_memq": jnp.tile(row(mem_q_norm), (1, 1, MEM_HEADS)),
        "g_mem": row(mem_norm),
        "w_memkv": w_mem_kv.astype(BF16),
        "g_memk": row(mem_k_norm),
        "w_branch": w_branch.astype(BF16),
        "w_out": w_out.astype(BF16),
        "g_ffn": row(ffn_norm),
        "w_ff1": w_ff1.astype(BF16),
        "w_ff2": w_ff2.astype(BF16),
    }


def _encoder_layer(x, mem, p, layer):
    batch, seq, _ = x.shape
    xf = x.reshape(batch * seq, D_MODEL)
    outs = _in_proj(xf, p, layer, batch, seq)
    dil_qkv, (qm, km, vm, mq) = outs[:3 * N_DIL], outs[3 * N_DIL:]
    a = _dilated(dil_qkv, batch, seq)
    b = _mla(qm, km, vm, batch, seq)
    kmem, vmem = _mem_kv(mem, p, layer)
    return _post(xf, a, b, mq, kmem, vmem, p, layer).reshape(batch, seq, D_MODEL)


def kernel(x_prompt, x_sample, mem_prompt, mem_sample, mix_norm, w_in, dil_q_norm, dil_k_norm, mla_q_a_norm, mla_kv_a_norm, w_mla_q_b, w_mla_kv_b, mla_q_norm, mla_k_norm, mem_norm, w_mem_kv, mem_q_norm, mem_k_norm, w_branch, w_out, ffn_norm, w_ff1, w_ff2):
    weights = (mix_norm, w_in, dil_q_norm, dil_k_norm, mla_q_a_norm, mla_kv_a_norm, w_mla_q_b,
               w_mla_kv_b, mla_q_norm, mla_k_norm, mem_norm, w_mem_kv, mem_q_norm, mem_k_norm,
               w_branch, w_out, ffn_norm, w_ff1, w_ff2)
    yp, ys = x_prompt, x_sample
    p = _prep_weights(*weights)
    names = ("rope_c", "rope_s1", "rope_s2")
    p_prompt = {**p, **dict(zip(names, _rope_tables(yp.shape[1])))}
    p_sample = {**p, **dict(zip(names, _rope_tables(ys.shape[1])))}
    for layer in range(mix_norm.shape[0]):
        yp = _encoder_layer(yp, mem_prompt, p_prompt, layer)
        ys = _encoder_layer(ys, mem_sample, p_sample, layer)
    return (yp, ys)
```

```python
import functools
import math

import numpy as np
import jax
import jax.numpy as jnp
from jax import lax
from jax.experimental import pallas as pl
from jax.experimental.pallas import tpu as pltpu

F32 = jnp.float32
BF16 = jnp.bfloat16

LANE = 128
VMEM_LIMIT_BYTES = 56 * 1024 * 1024

D_MODEL = 1024
EPS = 1e-6
NEG_INF = -1e30
LOG2E = math.log2(math.e)
DIL_GROUPS = ((128, 1), (512, 4), (2048, 16))
N_DIL = 3
DIL_HEADS = 4
DIL_HEAD_DIM = 128
DIL_WIDTH = N_DIL * DIL_HEADS * DIL_HEAD_DIM
DIL_OUT = DIL_HEADS * DIL_HEAD_DIM
MLA_HEADS = 8
MLA_Q_LORA = 384
MLA_KV_LORA = 256
MLA_NOPE = 64
MLA_ROPE = 32
MLA_QK = MLA_NOPE + MLA_ROPE
MLA_V = 64
MLA_PAD = MLA_HEADS * LANE
ROPE_THETA = 10000.0
MEM_HEADS = 4
MEM_HEAD_DIM = 128
MEM_WIDTH = MEM_HEADS * MEM_HEAD_DIM
N_BRANCH = 3
BRANCH_WIDTH = 512
D_FF = 4 * D_MODEL
GATE_WIDTH = N_BRANCH * D_MODEL

COL_DQ = 0
COL_DK = COL_DQ + DIL_WIDTH
COL_DV = COL_DK + DIL_WIDTH
COL_CQ = COL_DV + DIL_WIDTH
COL_CKV = COL_CQ + MLA_Q_LORA
COL_KR = COL_CKV + MLA_KV_LORA
COL_MQ = COL_KR + LANE
COL_END = COL_MQ + MEM_WIDTH

DIL_R = 64
DIL_SUB = 128
DIL_TOKENS = 1024
DIL_BATCH_ROWS = 256
DIL_QSCALE = LOG2E / math.sqrt(DIL_HEAD_DIM)
MLA_QSCALE = LOG2E / math.sqrt(MLA_QK)
MEM_QSCALE = LOG2E / math.sqrt(MEM_HEAD_DIM)

IN_TILE = 512
IN_SUB = 256
POST_TILE = 512
MLA_TQ = 256
MLA_QBLOCK = 512
MLA_TK = 512
MLA_GROUP = 8
FF_CHUNK = 1024

NT_DIMS = (((1,), (1,)), ((), ()))
TN_DIMS = (((0,), (0,)), ((), ()))


def _alibi_slopes():
    n = N_DIL * DIL_HEADS
    s = np.array([2.0 ** (-8.0 * (k + 1) / n) for k in range(n)], dtype=np.float32)
    return s.reshape(N_DIL, DIL_HEADS)


def _rinv(x, width):
    return lax.rsqrt(jnp.sum(x * x, axis=-1, keepdims=True) * (1.0 / width) + EPS)


def _resident(layer, shape):
    nd = len(shape)
    return pl.BlockSpec((None,) + tuple(shape), lambda *_: (layer,) + (0,) * nd, pipeline_mode=pl.Buffered(1))


def _params(semantics):
    return pltpu.CompilerParams(dimension_semantics=semantics, vmem_limit_bytes=VMEM_LIMIT_BYTES)


def _rope(t, c, s1, s2):
    return t * c + pltpu.roll(t, LANE - MLA_ROPE // 2, 1) * s1 + pltpu.roll(t, MLA_ROPE // 2, 1) * s2


def _in_kernel(x_ref, gmix_ref, w_ref, gdq_ref, gdk_ref, gqa_ref, gkva_ref, wqb_ref, wkvb_ref,
               gmq_ref, gmk_ref, gmemq_ref, rc_ref, rs1_ref, rs2_ref, *refs):
    for sub in range(x_ref.shape[0] // IN_SUB):
        _in_rows(sub, x_ref, gmix_ref, w_ref, gdq_ref, gdk_ref, gqa_ref, gkva_ref, wqb_ref, wkvb_ref,
                 gmq_ref, gmk_ref, gmemq_ref, rc_ref, rs1_ref, rs2_ref, *refs)


def _in_rows(sub, x_ref, gmix_ref, w_ref, gdq_ref, gdk_ref, gqa_ref, gkva_ref, wqb_ref, wkvb_ref,
             gmq_ref, gmk_ref, gmemq_ref, rc_ref, rs1_ref, rs2_ref, *refs):
    dil_refs, (qm_ref, km_ref, vm_ref, mq_ref) = refs[:3 * N_DIL], refs[3 * N_DIL:3 * N_DIL + 4]
    stage_refs = refs[3 * N_DIL + 4:]
    tm = IN_SUB
    rows = slice(sub * tm, (sub + 1) * tm)
    x = x_ref[rows, :]
    h = (x * _rinv(x, D_MODEL) * gmix_ref[...]).astype(BF16)

    def proj(lo, hi):
        return jnp.dot(h, w_ref[:, lo:hi], preferred_element_type=F32)

    def head_norm(blk, gain, scale):
        return blk * _rinv(blk, LANE) * gain * scale

    def emit_dilated(z, t, gains_ref, scale):
        for g, (_, dilation) in enumerate(DIL_GROUPS):
            out_ref = dil_refs[t * N_DIL + g]
            stage = None if dilation == 1 else stage_refs[t * (N_DIL - 1) + g - 1]
            for hd in range(DIL_HEADS):
                col = (g * DIL_HEADS + hd) * LANE
                blk = z[:, col:col + LANE]
                if gains_ref is not None:
                    blk = head_norm(blk, gains_ref[:, col:col + LANE], scale)
                if dilation == 1:
                    out_ref[rows, hd * LANE:(hd + 1) * LANE] = blk.astype(BF16)
                else:
                    stage[hd] = blk
            if dilation > 1:
                per = tm // dilation
                for r in range(dilation):
                    for hd in range(DIL_HEADS):
                        picked = stage[hd, pl.ds(r, per, stride=dilation), :]
                        out_ref[r, sub * per:(sub + 1) * per, hd * LANE:(hd + 1) * LANE] = picked.astype(BF16)

    rc, rs1, rs2 = rc_ref[rows, :], rs1_ref[rows, :], rs2_ref[rows, :]
    cq = proj(COL_CQ, COL_CKV)
    ckv = proj(COL_CKV, COL_KR)
    k_rope = proj(COL_KR, COL_MQ)
    cq = (cq * _rinv(cq, MLA_Q_LORA) * gqa_ref[...]).astype(BF16)
    ckv = (ckv * _rinv(ckv, MLA_KV_LORA) * gkva_ref[...]).astype(BF16)
    q = jnp.dot(cq, wqb_ref[...], preferred_element_type=F32)
    kv = jnp.dot(ckv, wkvb_ref[...], preferred_element_type=F32)
    ones_lane = (lax.broadcasted_iota(jnp.int32, (1, LANE), 1) == MLA_V).astype(F32)

    tokens = []

    def latent_heads(lo, hi):
        for hd in range(lo, hi):
            sl = slice(hd * LANE, (hd + 1) * LANE)
            qh = q[:, sl]
            qh = qh * _rinv(qh, MLA_QK) * gmq_ref[...]
            qm_ref[rows, sl] = (_rope(qh, rc, rs1, rs2) * MLA_QSCALE).astype(BF16)
            kh = kv[:, sl] + k_rope
            kh = kh * _rinv(kh, MLA_QK) * gmk_ref[...]
            k_out = _rope(kh, rc, rs1, rs2)
            km_ref[rows, sl] = k_out.astype(BF16)
            tokens.append(k_out[:8])
            vm_ref[rows, sl] = (kv[:, MLA_PAD + hd * LANE:MLA_PAD + (hd + 1) * LANE] + ones_lane).astype(BF16)

    third = MLA_HEADS // 3
    emit_dilated(proj(COL_DQ, COL_DK), 0, gdq_ref, DIL_QSCALE)
    latent_heads(0, third)
    emit_dilated(proj(COL_DK, COL_DV), 1, gdk_ref, 1.0)
    latent_heads(third, 2 * third)
    zmq = proj(COL_MQ, COL_END)
    for hd in range(MEM_HEADS):
        sl = slice(hd * LANE, (hd + 1) * LANE)
        mq_ref[rows, sl] = head_norm(zmq[:, sl], gmemq_ref[:, sl], MEM_QSCALE).astype(BF16)
    latent_heads(2 * third, MLA_HEADS)
    bits = pltpu.bitcast(tokens[0], jnp.uint32)
    for t in tokens[1:]:
        bits = bits | pltpu.bitcast(t, jnp.uint32)
    zero = pltpu.bitcast((bits >> 16) >> 16, F32)[:1].astype(BF16)
    h = h + jnp.concatenate([zero] * (D_MODEL // LANE), axis=-1)
    emit_dilated(proj(COL_DV, COL_CQ), 2, None, 1.0)


def _in_proj(xf, p, layer, batch, seq):
    tokens = xf.shape[0]
    tm = IN_TILE
    res = functools.partial(_resident, layer)
    pos_blocks = seq // tm
    row = lambda w: pl.BlockSpec((tm, w), lambda i: (i, 0))
    rope_spec = pl.BlockSpec((tm, LANE), lambda i: (i % pos_blocks, 0))
    out_specs, out_shape, scratch = [], [], []
    for _ in range(3):
        for _, dilation in DIL_GROUPS:
            if dilation == 1:
                out_specs.append(row(DIL_OUT))
                out_shape.append(jax.ShapeDtypeStruct((tokens, DIL_OUT), BF16))
            else:
                assert IN_SUB % (dilation * 16) == 0
                out_specs.append(pl.BlockSpec((None, dilation, tm // dilation, DIL_OUT),
                                              lambda i: (i // pos_blocks, 0, i % pos_blocks, 0)))
                out_shape.append(jax.ShapeDtypeStruct((batch, dilation, seq // dilation, DIL_OUT), BF16))
                scratch.append(pltpu.VMEM((DIL_HEADS, IN_SUB, LANE), F32))
    for w in (MLA_PAD, MLA_PAD, MLA_PAD, MEM_WIDTH):
        out_specs.append(row(w))
        out_shape.append(jax.ShapeDtypeStruct((tokens, w), BF16))
    return pl.pallas_call(
        _in_kernel,
        grid=(tokens // tm,),
        in_specs=[row(D_MODEL), res((1, D_MODEL)), res((D_MODEL, COL_END)),
                  res((1, DIL_WIDTH)), res((1, DIL_WIDTH)),
                  res((1, MLA_Q_LORA)), res((1, MLA_KV_LORA)),
                  res((MLA_Q_LORA, MLA_PAD)), res((MLA_KV_LORA, 2 * MLA_PAD)),
                  res((1, LANE)), res((1, LANE)), res((1, MEM_WIDTH)),
                  rope_spec, rope_spec, rope_spec],
        out_specs=out_specs,
        out_shape=out_shape,
        scratch_shapes=scratch,
        compiler_params=_params(("parallel",)),
        name="in_proj",
    )(xf, p["g_mix"], p["w_in"], p["g_dq"], p["g_dk"], p["g_qa"], p["g_kva"], p["w_qb"], p["w_kvb"],
      p["g_mq"], p["g_mk"], p["g_memq"], p["rope_c"], p["rope_s1"], p["rope_s2"])


def _dil_group(qkv_refs, state, o_ref, *, first, slopes, dilation, sub_len, tl, sub):
    q_ref, kp_ref, kc_ref, kn_ref, vp_ref, vc_ref, vn_ref = qkv_refs
    acc_st, m_st, d_st = state
    last = o_ref is not None

    def window(prev_ref, cur_ref, next_ref, r, r0, sl):
        lo, hi = r0 - DIL_R, r0 + sub + DIL_R
        parts = [prev_ref[r, :, sl]] if lo < 0 else []
        parts.append(cur_ref[r, max(lo, 0):min(hi, tl), sl])
        if hi > tl:
            parts.append(next_ref[r, :, sl])
        return parts[0] if len(parts) == 1 else jnp.concatenate(parts, axis=0)

    nkeys = sub + 2 * DIL_R
    qi = lax.broadcasted_iota(jnp.int32, (sub, nkeys), 0)
    kj = lax.broadcasted_iota(jnp.int32, (sub, nkeys), 1)
    absrel = jnp.abs(kj - DIL_R - qi)
    band = absrel <= DIL_R
    dist = absrel.astype(F32)
    block_start = pl.program_id(1) * tl
    n_sub = tl // sub
    bias_cache = {}

    def masked_bias(sb, hd):
        edge = sb == 0 or sb == n_sub - 1
        key = (sb if edge else -1, hd)
        if key not in bias_cache:
            ok = band
            if edge:
                upos = kj + (block_start + (sb * sub - DIL_R))
                ok = band & (upos >= 0) & (upos < sub_len)
            bias_cache[key] = jnp.where(ok, dist * (-float(slopes[hd]) * dilation * LOG2E), NEG_INF)
        return bias_cache[key]

    units = [(r, sb) for r in range(dilation) for sb in range(tl // sub)]
    batch = max(1, DIL_BATCH_ROWS // sub)
    cols = lambda hd: slice(hd * LANE, (hd + 1) * LANE)

    def score_stage(blocks):
        out = []
        for r, sb, hd in blocks:
            r0 = sb * sub
            s = lax.dot_general(q_ref[r, r0:r0 + sub, cols(hd)], window(kp_ref, kc_ref, kn_ref, r, r0, cols(hd)),
                                NT_DIMS, preferred_element_type=F32)
            out.append(s + masked_bias(sb, hd))
        return out

    def softmax_stage(blocks, scores):
        out = []
        for s in scores:
            m = jnp.max(s, axis=-1, keepdims=True)
            pr = jnp.exp2(s - m)
            out.append((m, pr.astype(BF16)))
        return out

    def value_stage(blocks, stats):
        out = []
        ones = jnp.ones((nkeys, LANE), BF16)
        for (r, sb, hd), (m, pr) in zip(blocks, stats):
            r0 = sb * sub
            v_ext = jnp.concatenate([window(vp_ref, vc_ref, vn_ref, r, r0, cols(hd)), ones], axis=-1)
            res = jnp.dot(pr, v_ext, preferred_element_type=F32)
            out.append((res[:, :LANE], m, res[:, LANE:]))
        return out

    def merge_stage(blocks, outs):
        for (r, sb, hd), (acc, m, den) in zip(blocks, outs):
            r0 = sb * sub
            if dilation == 1:
                rows = slice(r0, r0 + sub)
            else:
                rows = pl.ds(r0 * dilation + r, sub, stride=dilation)
            if not first:
                m_prev = m_st[hd, rows, :]
                m_new = jnp.maximum(m_prev, m)
                w_prev = jnp.exp2(m_prev - m_new)
                w_cur = jnp.exp2(m - m_new)
                acc = w_prev * acc_st[hd, rows, :] + w_cur * acc
                den = w_prev * d_st[hd, rows, :] + w_cur * den
                m = m_new
            if last:
                o_ref[hd, rows, :] = acc / den
            else:
                acc_st[hd, rows, :] = acc
                m_st[hd, rows, :] = jnp.broadcast_to(m, (sub, LANE))
                d_st[hd, rows, :] = jnp.broadcast_to(den, (sub, LANE))

    stages = (score_stage, softmax_stage, value_stage, merge_stage)
    return [(stages, [(r, sb, hd) for r, sb in units[u0:u0 + batch] for hd in range(DIL_HEADS)])
            for u0 in range(0, len(units), batch)]


def _dil_geometry(g, seq):
    window, dilation = DIL_GROUPS[g]
    assert window // (2 * dilation) == DIL_R
    sub_len = seq // dilation
    assert sub_len * dilation == seq and sub_len % DIL_R == 0
    tl = min(sub_len, DIL_TOKENS // dilation)
    sub = min(tl, DIL_SUB)
    assert sub_len % tl == 0 and tl % sub == 0 and tl % DIL_R == 0
    return dict(slopes=_alibi_slopes()[g], dilation=dilation, sub_len=sub_len, tl=tl, sub=sub)


def _dil_kernel(*refs, geometry):
    n_in = 7 * N_DIL
    o_ref = refs[n_in]
    state = refs[n_in + 1:]
    work = []
    for g, geo in enumerate(geometry):
        work += _dil_group(refs[7 * g:7 * g + 7], state, o_ref if g == N_DIL - 1 else None, first=g == 0, **geo)
    n_stage = 4
    carried = [None] * len(work)
    for t in range(len(work) + n_stage - 1):
        for s in reversed(range(n_stage)):
            i = t - s
            if 0 <= i < len(work):
                stages, blocks = work[i]
                carried[i] = stages[s](blocks) if s == 0 else stages[s](blocks, carried[i])


def _dilated(dil_qkv, batch, seq):
    block = min(seq, DIL_TOKENS)
    geometry = [_dil_geometry(g, seq) for g in range(N_DIL)]
    args, in_specs = [], []
    for g, geo in enumerate(geometry):
        dilation, tl, sub_len = geo["dilation"], geo["tl"], geo["sub_len"]
        assert tl * dilation == block
        edge_blocks = sub_len // DIL_R
        per_tl = tl // DIL_R
        cur_spec = pl.BlockSpec((None, dilation, tl, DIL_OUT), lambda b, i: (b, 0, i, 0))
        prev_spec = pl.BlockSpec((None, dilation, DIL_R, DIL_OUT),
                                 lambda b, i, per_tl=per_tl: (b, 0, jnp.maximum(i * per_tl - 1, 0), 0))
        next_spec = pl.BlockSpec(
            (None, dilation, DIL_R, DIL_OUT),
            lambda b, i, per_tl=per_tl, edge_blocks=edge_blocks:
                (b, 0, jnp.minimum((i + 1) * per_tl, edge_blocks - 1), 0))
        q, k, v = (dil_qkv[t * N_DIL + g].reshape(batch, dilation, sub_len, DIL_OUT) for t in range(3))
        args += [q, k, k, k, v, v, v]
        in_specs += [cur_spec, prev_spec, cur_spec, next_spec, prev_spec, cur_spec, next_spec]
    scratch = [pltpu.VMEM((DIL_HEADS, block, LANE), F32)] * 3
    return pl.pallas_call(
        functools.partial(_dil_kernel, geometry=geometry),
        grid=(batch, seq // block),
        in_specs=in_specs,
        out_specs=pl.BlockSpec((None, DIL_HEADS, block, LANE), lambda b, i: (b, 0, i, 0)),
        out_shape=jax.ShapeDtypeStruct((batch, DIL_HEADS, seq, LANE), F32),
        scratch_shapes=scratch,
        compiler_params=_params(("parallel", "parallel")),
        name="dilated_attn",
    )(*args)


def _mla_kernel(q_ref, k_ref, v_ref, o_ref, s_buf, p_buf, *, tq, tk, nk):
    assert nk >= 2 and nk % 2 == 0

    def sub_tile(sub, carry):
        _mla_rows(pl.ds(pl.multiple_of(sub * tq, tq), tq), q_ref, k_ref, v_ref, o_ref, s_buf, p_buf,
                  tq=tq, tk=tk, nk=nk)
        return carry

    lax.fori_loop(0, q_ref.shape[0] // tq, sub_tile, 0)


def _mla_rows(rows, q_ref, k_ref, v_ref, o_ref, s_buf, p_buf, *, tq, tk, nk):
    for grp in range(MLA_HEADS // MLA_GROUP):
        cols = [slice(hd * LANE, (hd + 1) * LANE) for hd in range(grp * MLA_GROUP, (grp + 1) * MLA_GROUP)]
        qs = [q_ref[rows, sl] for sl in cols]

        def tile(j):
            return slice(j * tk, (j + 1) * tk)

        def scores(j, h, slot):
            s = lax.dot_general(k_ref[tile(j), cols[h]], qs[h], NT_DIMS, preferred_element_type=F32)
            s_buf[slot, h] = s
            return jnp.max(s, axis=0, keepdims=True)

        def numerators(h, slot, m, tile_max):
            m_new = jnp.maximum(m, tile_max)
            p_buf[slot, h] = jnp.exp2(s_buf[slot, h] - m_new).astype(BF16)
            return m_new, jnp.exp2(m - m_new)

        def accumulate(j, h, slot, alpha, acc):
            pv = lax.dot_general(v_ref[tile(j), cols[h]], p_buf[slot, h], TN_DIMS,
                                 preferred_element_type=F32)
            return alpha * acc + pv

        def trip(j, slot, state):
            out = []
            for h, (m, alpha, tile_max, acc) in enumerate(state):
                acc = accumulate(j - 1, h, 1 - slot, alpha, acc)
                m, alpha = numerators(h, slot, m, tile_max)
                out.append((m, alpha, scores(j + 1, h, 1 - slot), acc))
            return tuple(out)

        state = []
        for h in range(MLA_GROUP):
            m0 = jnp.full((1, tq), -jnp.inf, F32)
            m, alpha = numerators(h, 0, m0, scores(0, h, 0))
            state.append((m, alpha, scores(1, h, 1), jnp.zeros((LANE, tq), F32)))

        for j in range(1, nk - 1):
            state = trip(j, j % 2, state)
        last = (nk - 1) % 2
        tops = []
        for h, (m, alpha, tile_max, acc) in enumerate(state):
            acc = accumulate(nk - 2, h, 1 - last, alpha, acc)
            m, alpha = numerators(h, last, m, tile_max)
            acc = accumulate(nk - 1, h, last, alpha, acc)
            tops.append((acc / acc[MLA_V:MLA_V + 1, :])[:MLA_V])
        for i in range(MLA_GROUP // 2):
            col = (grp * MLA_GROUP // 2 + i) * LANE
            o_ref[rows, col:col + LANE] = jnp.concatenate(tops[2 * i:2 * i + 2], axis=0).T.astype(BF16)


def _mla(qm, km, vm, batch, seq):
    tq, tk = MLA_TQ, MLA_TK
    view = lambda t: t.reshape(batch, seq, MLA_PAD)
    kv_spec = pl.BlockSpec((None, seq, MLA_PAD), lambda b, i: (b, 0, 0))
    out = pl.pallas_call(
        functools.partial(_mla_kernel, tq=tq, tk=tk, nk=seq // tk),
        grid=(batch, seq // MLA_QBLOCK),
        in_specs=[pl.BlockSpec((None, MLA_QBLOCK, MLA_PAD), lambda b, i: (b, i, 0)), kv_spec, kv_spec],
        out_specs=pl.BlockSpec((None, MLA_QBLOCK, MLA_HEADS * MLA_V), lambda b, i: (b, i, 0)),
        out_shape=jax.ShapeDtypeStruct((batch, seq, MLA_HEADS * MLA_V), BF16),
        scratch_shapes=[pltpu.VMEM((2, MLA_GROUP, tk, tq), F32), pltpu.VMEM((2, MLA_GROUP, tk, tq), BF16)],
        compiler_params=_params(("parallel", "arbitrary")),
        name="latent_attn",
    )(view(qm), view(km), view(vm))
    return out.reshape(batch * seq, MLA_HEADS * MLA_V)


def _memkv_kernel(mem_ref, g_ref, w_ref, gk_ref, k_ref, v_ref):
    x = mem_ref[...]
    h = (x * _rinv(x, D_MODEL) * g_ref[...]).astype(BF16)
    kv = jnp.dot(h, w_ref[...], preferred_element_type=F32)
    for hd in range(MEM_HEADS):
        sl = slice(hd * LANE, (hd + 1) * LANE)
        blk = kv[:, sl]
        k_ref[:, sl] = (blk * _rinv(blk, MEM_HEAD_DIM) * gk_ref[...]).astype(BF16)
    v_ref[...] = kv[:, MEM_WIDTH:].astype(BF16)


def _mem_kv(mem, p, layer):
    batch, n_mem, _ = mem.shape
    res = functools.partial(_resident, layer)
    spec = lambda w: pl.BlockSpec((None, n_mem, w), lambda b: (b, 0, 0))
    return pl.pallas_call(
        _memkv_kernel,
        grid=(batch,),
        in_specs=[spec(D_MODEL), res((1, D_MODEL)), res((D_MODEL, 2 * MEM_WIDTH)), res((1, LANE))],
        out_specs=[spec(MEM_WIDTH), spec(MEM_WIDTH)],
        out_shape=[jax.ShapeDtypeStruct((batch, n_mem, MEM_WIDTH), BF16)] * 2,
        compiler_params=_params(("parallel",)),
        name="memory_kv",
    )(mem, p["g_mem"], p["w_memkv"], p["g_memk"])


def _memory_attention(q, k_ref, v_ref):
    outs = []
    for hd in range(MEM_HEADS):
        sl = slice(hd * LANE, (hd + 1) * LANE)
        s = lax.dot_general(q[:, sl], k_ref[:, sl], NT_DIMS, preferred_element_type=F32)
        m = jnp.max(s, axis=-1, keepdims=True)
        pr = jnp.exp2(s - m)
        den = jnp.sum(pr, axis=-1, keepdims=True)
        o = jnp.dot(pr.astype(BF16), v_ref[:, sl], preferred_element_type=F32) / den
        outs.append(o.astype(BF16))
    return jnp.concatenate(outs, axis=-1)


def _post_kernel(x_ref, a_ref, b_ref, mq_ref, kmem_ref, vmem_ref, gmix_ref, wg_ref, wb_ref, wo_ref, gffn_ref,
                 w1_ref, w2_ref, o_ref):
    x = x_ref[...]
    h = (x * _rinv(x, D_MODEL) * gmix_ref[...]).astype(BF16)
    a = jnp.concatenate([a_ref[hd] for hd in range(DIL_HEADS)], axis=-1).astype(BF16)
    mixed = None
    m = _memory_attention(mq_ref[...], kmem_ref, vmem_ref)
    for i, br in enumerate((a, b_ref[...], m)):
        proj = jnp.dot(br, wb_ref[i], preferred_element_type=F32)
        logits = jnp.dot(h, wg_ref[:, i * D_MODEL:(i + 1) * D_MODEL], preferred_element_type=F32)
        gate = jax.nn.sigmoid(logits)
        mixed = gate * proj if mixed is None else mixed + gate * proj
    x1 = x + jnp.dot(mixed.astype(BF16), wo_ref[...], preferred_element_type=F32)
    h2 = (x1 * _rinv(x1, D_MODEL) * gffn_ref[...]).astype(BF16)
    acc = x1
    for c in range(D_FF // FF_CHUNK):
        sl = slice(c * FF_CHUNK, (c + 1) * FF_CHUNK)
        u = jnp.dot(h2, w1_ref[:, sl], preferred_element_type=F32)
        u = jnp.square(jnp.maximum(u, 0.0)).astype(BF16)
        acc = acc + jnp.dot(u, w2_ref[sl, :], preferred_element_type=F32)
    o_ref[...] = acc


def _post(xf, a, b, mq, kmem, vmem, p, layer):
    tokens = xf.shape[0]
    tm = POST_TILE
    res = functools.partial(_resident, layer)
    row = lambda w: pl.BlockSpec((tm, w), lambda i: (i, 0))
    seq_blocks = a.shape[2] // tm
    a_spec = pl.BlockSpec((None, DIL_HEADS, tm, LANE), lambda i: (i // seq_blocks, 0, i % seq_blocks, 0))
    mem_spec = pl.BlockSpec((None, kmem.shape[1], MEM_WIDTH), lambda i: (i // seq_blocks, 0, 0))
    return pl.pallas_call(
        _post_kernel,
        grid=(tokens // tm,),
        in_specs=[row(D_MODEL), a_spec, row(BRANCH_WIDTH), row(MEM_WIDTH), mem_spec, mem_spec,
                  res((1, D_MODEL)), res((D_MODEL, GATE_WIDTH)),
                  res((N_BRANCH, BRANCH_WIDTH, D_MODEL)), res((D_MODEL, D_MODEL)),
                  res((1, D_MODEL)), res((D_MODEL, D_FF)), res((D_FF, D_MODEL))],
        out_specs=row(D_MODEL),
        out_shape=jax.ShapeDtypeStruct((tokens, D_MODEL), F32),
        compiler_params=_params(("parallel",)),
        name="merge_ffn",
    )(xf, a, b, mq, kmem, vmem, p["g_mix"], p["w_gate"], p["w_branch"], p["w_out"], p["g_ffn"], p["w_ff1"], p["w_ff2"])


def _rope_tables(seq):
    half = MLA_ROPE // 2
    inv = ROPE_THETA ** (-jnp.arange(half, dtype=F32) * 2.0 / MLA_ROPE)
    ang = jnp.arange(seq, dtype=F32)[:, None] * inv[None, :]
    c, s = jnp.cos(ang), jnp.sin(ang)
    z = lambda w: jnp.zeros((seq, w), F32)
    tail = LANE - MLA_QK
    rope_c = jnp.concatenate([jnp.ones((seq, MLA_NOPE), F32), c, c, z(tail)], axis=-1)
    rope_s1 = jnp.concatenate([z(MLA_NOPE), -s, z(half), z(tail)], axis=-1)
    rope_s2 = jnp.concatenate([z(MLA_NOPE), z(half), s, z(tail)], axis=-1)
    return rope_c, rope_s1, rope_s2


def _pad_lanes(t, width):
    return jnp.pad(t, [(0, 0)] * (t.ndim - 1) + [(0, width - t.shape[-1])])


def _prep_weights(mix_norm, w_in, dil_q_norm, dil_k_norm, mla_q_a_norm, mla_kv_a_norm, w_mla_q_b,
                  w_mla_kv_b, mla_q_norm, mla_k_norm, mem_norm, w_mem_kv, mem_q_norm, mem_k_norm,
                  w_branch, w_out, ffn_norm, w_ff1, w_ff2):
    depth = w_in.shape[0]
    o_cq = 3 * DIL_WIDTH
    o_ckv = o_cq + MLA_Q_LORA
    o_kr = o_ckv + MLA_KV_LORA
    o_mq = o_kr + MLA_ROPE
    o_gl = o_mq + MEM_WIDTH
    kr = jnp.pad(w_in[:, :, o_kr:o_mq], ((0, 0), (0, 0), (MLA_NOPE, LANE - MLA_QK)))
    w_r = jnp.concatenate([w_in[:, :, :o_kr], kr, w_in[:, :, o_mq:o_gl]], axis=2).astype(BF16)
    w_qb = _pad_lanes(w_mla_q_b.reshape(depth, MLA_Q_LORA, MLA_HEADS, MLA_QK), LANE)
    w_kvb = w_mla_kv_b.reshape(depth, MLA_KV_LORA, MLA_HEADS, MLA_NOPE + MLA_V)
    w_kb = _pad_lanes(w_kvb[..., :MLA_NOPE], LANE).reshape(depth, MLA_KV_LORA, MLA_PAD)
    w_vb = _pad_lanes(w_kvb[..., MLA_NOPE:], LANE).reshape(depth, MLA_KV_LORA, MLA_PAD)
    row = lambda g: g.reshape(depth, 1, -1)
    return {
        "g_mix": row(mix_norm),
        "w_in": w_r,
        "w_gate": w_in[:, :, o_gl:].astype(BF16),
        "g_dq": row(dil_q_norm),
        "g_dk": row(dil_k_norm),
        "g_qa": row(mla_q_a_norm),
        "g_kva": row(mla_kv_a_norm),
        "w_qb": w_qb.reshape(depth, MLA_Q_LORA, MLA_PAD).astype(BF16),
        "w_kvb": jnp.concatenate([w_kb, w_vb], axis=2).astype(BF16),
        "g_mq": _pad_lanes(row(mla_q_norm), LANE),
        "g_mk": _pad_lanes(row(mla_k_norm), LANE),
        "g_memq": jnp.tile(row(mem_q_norm), (1, 1, MEM_HEADS)),
        "g_mem": row(mem_norm),
        "w_memkv": w_mem_kv.astype(BF16),
        "g_memk": row(mem_k_norm),
        "w_branch": w_branch.astype(BF16),
        "w_out": w_out.astype(BF16),
        "g_ffn": row(ffn_norm),
        "w_ff1": w_ff1.astype(BF16),
        "w_ff2": w_ff2.astype(BF16),
    }


def _encoder_layer(x, mem, p, layer):
    batch, seq, _ = x.shape
    xf = x.reshape(batch * seq, D_MODEL)
    outs = _in_proj(xf, p, layer, batch, seq)
    dil_qkv, (qm, km, vm, mq) = outs[:3 * N_DIL], outs[3 * N_DIL:]
    a = _dilated(dil_qkv, batch, seq)
    b = _mla(qm, km, vm, batch, seq)
    kmem, vmem = _mem_kv(mem, p, layer)
    return _post(xf, a, b, mq, kmem, vmem, p, layer).reshape(batch, seq, D_MODEL)


def kernel(x_prompt, x_sample, mem_prompt, mem_sample, mix_norm, w_in, dil_q_norm, dil_k_norm, mla_q_a_norm, mla_kv_a_norm, w_mla_q_b, w_mla_kv_b, mla_q_norm, mla_k_norm, mem_norm, w_mem_kv, mem_q_norm, mem_k_norm, w_branch, w_out, ffn_norm, w_ff1, w_ff2):
    weights = (mix_norm, w_in, dil_q_norm, dil_k_norm, mla_q_a_norm, mla_kv_a_norm, w_mla_q_b,
               w_mla_kv_b, mla_q_norm, mla_k_norm, mem_norm, w_mem_kv, mem_q_norm, mem_k_norm,
               w_branch, w_out, ffn_norm, w_ff1, w_ff2)
    yp, ys = x_prompt, x_sample
    p = _prep_weights(*weights)
    names = ("rope_c", "rope_s1", "rope_s2")
    p_prompt = {**p, **dict(zip(names, _rope_tables(yp.shape[1])))}
    p_sample = {**p, **dict(zip(names, _rope_tables(ys.shape[1])))}
    for layer in range(mix_norm.shape[0]):
        yp = _encoder_layer(yp, mem_prompt, p_prompt, layer)
        ys = _encoder_layer(ys, mem_sample, p_sample, layer)
    return (yp, ys)
```
